```python
import jax, jax.numpy as jnp
from jax import lax
import numpy as np

D_MODEL = 4096
BATCH = 4
SEQ = 2048
DEPTH = 4
DEC_BATCH = 8
DEC_SEQ = 1
PAST_LEN = 8192
PAGE_SIZE = 128

N_A_LAYERS = DEPTH // 2
N_B_LAYERS = DEPTH - N_A_LAYERS
POOL_WINDOWS = (2, 4, 8, 16)
N_POOL_GROUPS = len(POOL_WINDOWS)
POOL_GROUP_DIM = D_MODEL // N_POOL_GROUPS
POOL_STATE = max(POOL_WINDOWS) - 1
HEAD_DIM = 128
N_HEADS = D_MODEL // HEAD_DIM
MOBA_BLOCK = 256
MOBA_TOPK = 3
Q_CHUNK = 8
D_FF = 4 * D_MODEL
EPS = 1e-6

kernel_name = "pool_moba_yoco_decoder_step"


def rmsnorm(x, g):
    xf = x.astype(jnp.float32)
    r = lax.rsqrt(jnp.mean(xf * xf, axis=-1, keepdims=True) + EPS)
    return (xf * r * g.astype(jnp.float32)).astype(x.dtype)


def sqrelu_mlp(h, w_up, w_down):
    u = jnp.maximum(h @ w_up, 0)
    return (u * u) @ w_down


def pool_mixer(h, prev, pos0, w_map, scale):
    B, T, _ = h.shape
    ext = jnp.concatenate([prev.astype(h.dtype), h], axis=1)
    cz = jnp.concatenate([jnp.zeros((B, 1, D_MODEL), jnp.float32),
                          jnp.cumsum(ext.astype(jnp.float32), axis=1)], axis=1)
    pos = pos0 + jnp.arange(T, dtype=jnp.int32)
    hf = h.astype(jnp.float32)
    outs = []
    for g, w in enumerate(POOL_WINDOWS):
        c0, c1 = g * POOL_GROUP_DIM, (g + 1) * POOL_GROUP_DIM
        wsum = cz[:, POOL_STATE + 1:, c0:c1] - cz[:, POOL_STATE + 1 - w:POOL_STATE + 1 - w + T, c0:c1]
        cnt = jnp.minimum(pos + 1, w).astype(jnp.float32)[None, :, None]
        outs.append(wsum / cnt - hf[..., c0:c1])
    d = jnp.stack(outs, axis=2).astype(h.dtype)
    y = jnp.einsum('btgc,gce->btge', d, w_map).reshape(B, T, D_MODEL)
    return y * scale, ext[:, -POOL_STATE:]


def moba_chunk(q, pos, kb, vb, kmean):
    B, Qc, H, Dh = q.shape
    nb = kb.shape[1]
    qf = q.astype(jnp.float32)
    cur = pos // MOBA_BLOCK
    gate = jnp.einsum('bqhd,bnhd->bqhn', qf, kmean)
    past = jnp.arange(nb)[None, :] < cur[:, None]
    gate = jnp.where(past[None, :, None, :], gate, -jnp.inf)
    if nb < MOBA_TOPK:
        gate = jnp.pad(gate, ((0, 0), (0, 0), (0, 0), (0, MOBA_TOPK - nb)), constant_values=-jnp.inf)
    top_val, top_idx = lax.top_k(gate, MOBA_TOPK)
    sel_ok = jnp.isfinite(top_val)
    top_idx = jnp.where(sel_ok, top_idx, 0).astype(jnp.int32)
    own = jnp.broadcast_to(cur[None, :, None, None].astype(jnp.int32), (B, Qc, H, 1))
    slots = jnp.concatenate([top_idx, own], axis=-1)
    slot_ok = jnp.concatenate([sel_ok, jnp.ones((B, Qc, H, 1), bool)], axis=-1)
    bi = jnp.arange(B)[:, None, None, None]
    hi = jnp.arange(H)[None, None, :, None]
    ks = kb[bi, slots, :, hi, :]
    vs = vb[bi, slots, :, hi, :]
    logits = jnp.einsum('bqhd,bqhskd->bqhsk', qf, ks.astype(jnp.float32)) * (HEAD_DIM ** -0.5)
    kpos = slots[..., None] * MOBA_BLOCK + jnp.arange(MOBA_BLOCK, dtype=jnp.int32)
    ok = slot_ok[..., None] & (kpos <= pos[None, :, None, None, None])
    logits = jnp.where(ok, logits, -jnp.inf).reshape(B, Qc, H, -1)
    p = jax.nn.softmax(logits, axis=-1).reshape(B, Qc, H, MOBA_TOPK + 1, MOBA_BLOCK)
    out = jnp.einsum('bqhsk,bqhskd->bqhd', p, vs.astype(jnp.float32))
    return out.astype(q.dtype)


def moba_attention(q, pos0, k_all, v_all):
    B, T, H, Dh = q.shape
    L = k_all.shape[1]
    nb = -(-L // MOBA_BLOCK)
    pad = ((0, 0), (0, nb * MOBA_BLOCK - L), (0, 0), (0, 0))
    kb = jnp.pad(k_all, pad).reshape(B, nb, MOBA_BLOCK, H, Dh)
    vb = jnp.pad(v_all, pad).reshape(B, nb, MOBA_BLOCK, H, Dh)
    kmean = jnp.mean(kb.astype(jnp.float32), axis=2)
    qc = Q_CHUNK if T % Q_CHUNK == 0 else T
    nc = T // qc
    qs = q.reshape(B, nc, qc, H, Dh).transpose(1, 0, 2, 3, 4)
    ps = (pos0 + jnp.arange(T, dtype=jnp.int32)).reshape(nc, qc)
    out = lax.map(lambda a: moba_chunk(a[0], a[1], kb, vb, kmean), (qs, ps))
    return out.transpose(1, 0, 2, 3, 4).reshape(B, T, H, Dh)


def trunk(x, pos0, pool_prev, k_past, v_past, g_pool, w_pool, s_pool, g_mlp, w_up, w_down,
          g_kv, w_k, w_v, g_attn, w_q, w_o, g_final):
    B, T, _ = x.shape
    pool_new = []
    k_new = v_new = k_all = v_all = None
    for l in range(DEPTH):
        if l < N_A_LAYERS:
            y, st = pool_mixer(rmsnorm(x, g_pool[l]), pool_prev[l], pos0, w_pool[l], s_pool[l])
            x = x + y
            pool_new.append(st)
        else:
            j = l - N_A_LAYERS
            q = (rmsnorm(x, g_attn[j]) @ w_q[j]).reshape(B, T, N_HEADS, HEAD_DIM)
            o = moba_attention(q, pos0, k_all, v_all)
            x = x + o.reshape(B, T, N_HEADS * HEAD_DIM) @ w_o[j]
        x = x + sqrelu_mlp(rmsnorm(x, g_mlp[l]), w_up[l], w_down[l])
        if l == N_A_LAYERS - 1:
            hkv = rmsnorm(x, g_kv)
            k_new = (hkv @ w_k).reshape(B, T, N_HEADS, HEAD_DIM)
            v_new = (hkv @ w_v).reshape(B, T, N_HEADS, HEAD_DIM)
            k_all = jnp.concatenate([k_past.astype(k_new.dtype), k_new], axis=1)
            v_all = jnp.concatenate([v_past.astype(v_new.dtype), v_new], axis=1)
    return rmsnorm(x, g_final), jnp.stack(pool_new, axis=0), k_new, v_new


def setup_inputs(seed: int = 0) -> dict:
    key = jax.random.key(seed)
    ks = jax.random.split(key, 24)
    n_pages = PAST_LEN // PAGE_SIZE
    used = DEC_BATCH * n_pages
    n_phys = used + max(1, used // 4)
    hd = N_HEADS * HEAD_DIM

    def nrm(k, shape, s):
        return jax.random.normal(k, shape, jnp.float32) * s

    page_table = jax.random.permutation(ks[0], n_phys)[:used].reshape(DEC_BATCH, n_pages).astype(jnp.int32)
    return {
        "x_prompt": nrm(ks[1], (BATCH, SEQ, D_MODEL), 1.0),
        "x_sample": nrm(ks[2], (DEC_BATCH, DEC_SEQ, D_MODEL), 1.0),
        "cache_k": nrm(ks[3], (n_phys, PAGE_SIZE, N_HEADS, HEAD_DIM), 1.0),
        "cache_v": nrm(ks[4], (n_phys, PAGE_SIZE, N_HEADS, HEAD_DIM), 1.0),
        "state_pool": nrm(ks[5], (N_A_LAYERS, DEC_BATCH, POOL_STATE, D_MODEL), 1.0),
        "page_table": page_table,
        "g_pool": 1.0 + nrm(ks[6], (N_A_LAYERS, D_MODEL), 0.05),
        "w_pool": nrm(ks[7], (N_A_LAYERS, N_POOL_GROUPS, POOL_GROUP_DIM, POOL_GROUP_DIM), POOL_GROUP_DIM ** -0.5),
        "s_pool": 1.0 + nrm(ks[8], (N_A_LAYERS, D_MODEL), 0.05),
        "g_mlp": 1.0 + nrm(ks[9], (DEPTH, D_MODEL), 0.05),
        "w_up": nrm(ks[10], (DEPTH, D_MODEL, D_FF), D_MODEL ** -0.5),
        "w_down": nrm(ks[11], (DEPTH, D_FF, D_MODEL), D_FF ** -0.5),
        "g_kv": 1.0 + nrm(ks[12], (D_MODEL,), 0.05),
        "w_k": nrm(ks[13], (D_MODEL, hd), D_MODEL ** -0.5),
        "w_v": nrm(ks[14], (D_MODEL, hd), D_MODEL ** -0.5),
        "g_attn": 1.0 + nrm(ks[15], (N_B_LAYERS, D_MODEL), 0.05),
        "w_q": nrm(ks[16], (N_B_LAYERS, D_MODEL, hd), D_MODEL ** -0.5),
        "w_o": nrm(ks[17], (N_B_LAYERS, hd, D_MODEL), hd ** -0.5),
        "g_final": 1.0 + nrm(ks[18], (D_MODEL,), 0.05),
    }


def reference(x_prompt, x_sample, cache_k, cache_v, state_pool, page_table,
              g_pool, w_pool, s_pool, g_mlp, w_up, w_down, g_kv, w_k, w_v,
              g_attn, w_q, w_o, g_final):
    bp = x_prompt.shape[0]
    k_empty = jnp.zeros((bp, 0, N_HEADS, HEAD_DIM), x_prompt.dtype)
    pool_zero = jnp.zeros((N_A_LAYERS, bp, POOL_STATE, D_MODEL), x_prompt.dtype)
    y_prompt, pool_prompt, k_prompt, v_prompt = trunk(
        x_prompt, 0, pool_zero, k_empty, k_empty, g_pool, w_pool, s_pool, g_mlp, w_up, w_down,
        g_kv, w_k, w_v, g_attn, w_q, w_o, g_final)

    db, n_pages = page_table.shape
    past_len = n_pages * cache_k.shape[1]
    k_past = cache_k[page_table].reshape(db, past_len, N_HEADS, HEAD_DIM)
    v_past = cache_v[page_table].reshape(db, past_len, N_HEADS, HEAD_DIM)
    y_sample, pool_sample, k_sample, v_sample = trunk(
        x_sample, past_len, state_pool, k_past, v_past, g_pool, w_pool, s_pool, g_mlp, w_up, w_down,
        g_kv, w_k, w_v, g_attn, w_q, w_o, g_final)

    return (y_prompt, y_sample, k_prompt, v_prompt, pool_prompt, k_sample, v_sample, pool_sample)
```

```python
import functools

import jax
import jax.numpy as jnp
from jax import lax
from jax.experimental import pallas as pl
from jax.experimental.pallas import tpu as pltpu

EPS = 1e-6
POOL_WINDOWS = (2, 4, 8, 16)
POOL_STATE = max(POOL_WINDOWS) - 1
HALO = max(POOL_WINDOWS)
MOBA_BLOCK = 256
MOBA_TOPK = 3

BF16 = jnp.bfloat16
F32 = jnp.float32
NEG_INF = float("-inf")
INF = float("inf")
LANES = 128
SUBLANES = 8
VMEM_LIMIT = 60 * 1024 * 1024

_NT = (((1,), (1,)), ((), ()))


def _params(*semantics):
    return pltpu.CompilerParams(dimension_semantics=semantics, vmem_limit_bytes=VMEM_LIMIT)


def _rms(x, g):
    ms = jnp.mean(x * x, axis=-1, keepdims=True)
    return x * lax.rsqrt(ms + EPS) * g


def _mlp_kernel(x_ref, g_ref, gf_ref, wu_ref, wd_ref, o_ref, h_ref, *, final_norm):
    j = pl.program_id(1)

    @pl.when(j == 0)
    def _():
        x = x_ref[...]
        h_ref[...] = _rms(x, g_ref[...]).astype(BF16)
        o_ref[...] = x

    u = jnp.dot(h_ref[...], wu_ref[...], preferred_element_type=F32)
    u = jnp.maximum(u, 0.0)
    u = (u * u).astype(BF16)
    o_ref[...] += jnp.dot(u, wd_ref[...], preferred_element_type=F32)

    if final_norm:
        @pl.when(j == pl.num_programs(1) - 1)
        def _():
            o_ref[...] = _rms(o_ref[...], gf_ref[...])


def _mlp(x, g, gf, w_up, w_down, layer, *, final_norm, tm, tf):
    m, d = x.shape
    f = w_up.shape[2]
    tm = min(tm, m)
    tf = min(tf, f)
    return pl.pallas_call(
        functools.partial(_mlp_kernel, final_norm=final_norm),
        grid=(m // tm, f // tf),
        in_specs=[
            pl.BlockSpec((tm, d), lambda i, j: (i, 0), pipeline_mode=pl.Buffered(1)),
            pl.BlockSpec((1, d), lambda i, j: (0, 0)),
            pl.BlockSpec((1, d), lambda i, j: (0, 0)),
            pl.BlockSpec((None, d, tf), lambda i, j: (layer, 0, j)),
            pl.BlockSpec((None, tf, d), lambda i, j: (layer, j, 0)),
        ],
        out_specs=pl.BlockSpec((tm, d), lambda i, j: (i, 0)),
        out_shape=jax.ShapeDtypeStruct((m, d), F32),
        scratch_shapes=[pltpu.VMEM((tm, d), BF16)],
        compiler_params=_params("parallel", "arbitrary"),
        name="mlp",
    )(x, g, gf, w_up, w_down)


def _norm_proj_kernel(x_ref, g_ref, *refs, n):
    w_refs, o_refs, h_ref = refs[:n], refs[n:2 * n], refs[2 * n]

    @pl.when(pl.program_id(1) == 0)
    def _():
        h_ref[...] = _rms(x_ref[...], g_ref[...]).astype(BF16)

    h = h_ref[...]
    for w_ref, o_ref in zip(w_refs, o_refs):
        o_ref[...] = jnp.dot(h, w_ref[...], preferred_element_type=F32)


def _norm_proj(x, g, weights, *, tm, tn):
    m, d = x.shape
    nout = weights[0][0].shape[-1]
    tm = min(tm, m)
    tn = min(tn, nout)
    w_specs = []
    for w, idx in weights:
        if idx is None:
            w_specs.append(pl.BlockSpec((d, tn), lambda i, j: (0, j)))
        else:
            w_specs.append(pl.BlockSpec((None, d, tn), lambda i, j, idx=idx: (idx, 0, j)))
    n = len(weights)
    return pl.pallas_call(
        functools.partial(_norm_proj_kernel, n=n),
        grid=(m // tm, nout // tn),
        in_specs=[pl.BlockSpec((tm, d), lambda i, j: (i, 0)),
                  pl.BlockSpec((1, d), lambda i, j: (0, 0))] + w_specs,
        out_specs=[pl.BlockSpec((tm, tn), lambda i, j: (i, j))] * n,
        out_shape=[jax.ShapeDtypeStruct((m, nout), F32)] * n,
        scratch_shapes=[pltpu.VMEM((tm, d), BF16)],
        compiler_params=_params("parallel", "arbitrary"),
        name="norm_proj",
    )(x, g, *[w for w, _ in weights])


def _out_proj_kernel(a_ref, w_ref, x_ref, o_ref):
    a = a_ref[...].astype(BF16)
    o_ref[...] = x_ref[...] + jnp.dot(a, w_ref[...], preferred_element_type=F32)


def _out_proj(a, w_o, layer, x, *, tm, tn):
    m, d = x.shape
    k = a.shape[1]
    tm = min(tm, m)
    tn = min(tn, d)
    return pl.pallas_call(
        _out_proj_kernel,
        grid=(m // tm, d // tn),
        in_specs=[
            pl.BlockSpec((tm, k), lambda i, j: (i, 0)),
            pl.BlockSpec((None, k, tn), lambda i, j: (layer, 0, j)),
            pl.BlockSpec((tm, tn), lambda i, j: (i, j)),
        ],
        out_specs=pl.BlockSpec((tm, tn), lambda i, j: (i, j)),
        out_shape=jax.ShapeDtypeStruct((m, d), F32),
        compiler_params=_params("parallel", "arbitrary"),
        name="out_proj",
    )(a, w_o, x)


def _window_sums(ext, w, rows):
    s, span = ext, 1
    while span < w:
        s = s[span:] + s[:-span]
        span *= 2
    start = HALO + 1 - w
    return s[start:start + rows]


def _pool_prompt_kernel(x_ref, halo_ref, g_ref, w_ref, s_ref, o_ref, st_ref, *, tp):
    i = pl.program_id(1)
    x = x_ref[0]
    g = g_ref[...]
    h = _rms(x, g)
    hh = _rms(halo_ref[0], g)
    hh = jnp.where(i > 0, hh, 0.0)
    ext = jnp.concatenate([hh, h], axis=0)
    cg = x.shape[1] // len(POOL_WINDOWS)
    pos = i * tp + lax.broadcasted_iota(jnp.int32, (tp, 1), 0)
    for gi, w in enumerate(POOL_WINDOWS):
        c0, c1 = gi * cg, (gi + 1) * cg
        wsum = _window_sums(ext[:, c0:c1], w, tp)
        cnt = jnp.minimum(pos + 1, w).astype(F32)
        dlt = wsum * (1.0 / cnt) - h[:, c0:c1]
        y = jnp.dot(dlt.astype(BF16), w_ref[gi], preferred_element_type=F32)
        o_ref[0, :, c0:c1] = x[:, c0:c1] + y * s_ref[:, c0:c1]

    @pl.when(i == pl.num_programs(1) - 1)
    def _():
        st_ref[0] = h[tp - POOL_STATE:, :]


def _pool_prompt(x, g, w_pool, layer, s, *, tp):
    b, t, d = x.shape
    tp = min(tp, t)
    cg = d // len(POOL_WINDOWS)
    hb = tp // HALO
    return pl.pallas_call(
        functools.partial(_pool_prompt_kernel, tp=tp),
        grid=(b, t // tp),
        in_specs=[
            pl.BlockSpec((1, tp, d), lambda bi, i: (bi, i, 0)),
            pl.BlockSpec((1, HALO, d), lambda bi, i: (bi, jnp.maximum(i * hb - 1, 0), 0)),
            pl.BlockSpec((1, d), lambda bi, i: (0, 0)),
            pl.BlockSpec((None, len(POOL_WINDOWS), cg, cg), lambda bi, i: (layer, 0, 0, 0)),
            pl.BlockSpec((1, d), lambda bi, i: (0, 0)),
        ],
        out_specs=[
            pl.BlockSpec((1, tp, d), lambda bi, i: (bi, i, 0)),
            pl.BlockSpec((1, POOL_STATE, d), lambda bi, i: (bi, 0, 0)),
        ],
        out_shape=[jax.ShapeDtypeStruct((b, t, d), F32),
                   jax.ShapeDtypeStruct((b, POOL_STATE, d), F32)],
        compiler_params=_params("parallel", "arbitrary"),
        name="pool_prompt",
    )(x, x, g, w_pool, s)


def _pool_sample_kernel(x_ref, prev_ref, g_ref, w_ref, s_ref, o_ref, st_ref, d_ref, *, pos0):
    x = x_ref[...]
    h = _rms(x, g_ref[...])
    nb = x.shape[0]
    cg = x.shape[1] // len(POOL_WINDOWS)
    for b in range(nb):
        pv = prev_ref[b]
        hb = h[b:b + 1]
        st_ref[b, 0:POOL_STATE - 1, :] = pv[1:POOL_STATE]
        st_ref[b, POOL_STATE - 1:POOL_STATE, :] = hb
        for gi, w in enumerate(POOL_WINDOWS):
            c0, c1 = gi * cg, (gi + 1) * cg
            wsum = hb[:, c0:c1] + jnp.sum(pv[POOL_STATE + 1 - w:, c0:c1], axis=0, keepdims=True)
            cnt = float(min(pos0 + 1, w))
            d_ref[b:b + 1, c0:c1] = wsum / cnt - hb[:, c0:c1]
    for gi in range(len(POOL_WINDOWS)):
        c0, c1 = gi * cg, (gi + 1) * cg
        y = jnp.dot(d_ref[:, c0:c1].astype(BF16), w_ref[gi], preferred_element_type=F32)
        o_ref[:, c0:c1] = x[:, c0:c1] + y * s_ref[:, c0:c1]


def _pool_sample(x, prev, g, w_pool, layer, s, *, pos0):
    b, d = x.shape
    cg = d // len(POOL_WINDOWS)
    return pl.pallas_call(
        functools.partial(_pool_sample_kernel, pos0=pos0),
        grid=(1,),
        in_specs=[
            pl.BlockSpec((b, d), lambda i: (0, 0)),
            pl.BlockSpec((None, b, POOL_STATE, d), lambda i: (layer, 0, 0, 0)),
            pl.BlockSpec((1, d), lambda i: (0, 0)),
            pl.BlockSpec((None, len(POOL_WINDOWS), cg, cg), lambda i: (layer, 0, 0, 0)),
            pl.BlockSpec((1, d), lambda i: (0, 0)),
        ],
        out_specs=[
            pl.BlockSpec((b, d), lambda i: (0, 0)),
            pl.BlockSpec((b, POOL_STATE, d), lambda i: (0, 0, 0)),
        ],
        out_shape=[jax.ShapeDtypeStruct((b, d), F32),
                   jax.ShapeDtypeStruct((b, POOL_STATE, d), F32)],
        scratch_shapes=[pltpu.VMEM((b, d), F32)],
        compiler_params=_params("arbitrary"),
        name="pool_sample",
    )(x, prev, g, w_pool, s)


def _moba_prompt_kernel(q_ref, k_ref, v_ref, o_ref, *, scale):
    t = q_ref.shape[1]
    blk = MOBA_BLOCK
    nb = t // blk
    nbp = -(-nb // SUBLANES) * SUBLANES
    k = k_ref[0]
    kb = k.astype(BF16)
    vb = v_ref[0].astype(BF16)
    means = [jnp.mean(k[n * blk:(n + 1) * blk], axis=0, keepdims=True) for n in range(nb)]
    if nbp > nb:
        means.append(jnp.zeros((nbp - nb, k.shape[1]), F32))
    kmean = jnp.concatenate(means, axis=0)

    row = lax.broadcasted_iota(jnp.int32, (blk, blk), 0)
    col = lax.broadcasted_iota(jnp.int32, (blk, blk), 1)
    causal = col <= row
    eye = (row == col).astype(BF16)
    n_iota = lax.broadcasted_iota(jnp.int32, (nbp, blk), 0)

    for c in range(nb):
        qc = q_ref[0, c * blk:(c + 1) * blk, :]
        nk = (c + 1) * blk
        s = lax.dot_general(qc.astype(BF16), kb[:nk], _NT, preferred_element_type=F32) * scale
        parts = []
        if c > 0:
            gt = lax.dot_general(kmean, qc, _NT, precision=lax.Precision.HIGHEST,
                                 preferred_element_type=F32)
            past = n_iota < c
            gt = jnp.where(past, gt, NEG_INF)
            rank = jnp.zeros((nbp, blk), jnp.int32)
            for m in range(c):
                rm = gt[m:m + 1, :]
                beats = (rm > gt) | ((rm == gt) & (n_iota > m))
                rank = rank + beats.astype(jnp.int32)
            sel_t = past & (rank < MOBA_TOPK) & (jnp.abs(gt) < INF)
            sel_t = jnp.concatenate(
                [sel_t.astype(F32), jnp.zeros((LANES - nbp, blk), F32)], axis=0).astype(BF16)
            sel = lax.dot_general(eye, sel_t, _NT, preferred_element_type=F32)
            for n in range(c):
                keep = jnp.broadcast_to(sel[:, n:n + 1], (blk, blk)) > 0.5
                parts.append(jnp.where(keep, s[:, n * blk:(n + 1) * blk], NEG_INF))
        parts.append(jnp.where(causal, s[:, c * blk:], NEG_INF))
        sm = jnp.concatenate(parts, axis=1) if len(parts) > 1 else parts[0]
        mx = jnp.max(sm, axis=1, keepdims=True)
        p = jnp.exp(sm - mx)
        l = jnp.sum(p, axis=1, keepdims=True)
        o = jnp.dot(p.astype(BF16), vb[:nk], preferred_element_type=F32) * (1.0 / l)
        o_ref[0, c * blk:(c + 1) * blk, :] = o.astype(o_ref.dtype)


def _moba_prompt(q, k, v, *, heads):
    b, t, hd = q.shape
    dh = hd // heads
    spec = pl.BlockSpec((1, t, dh), lambda bi, h: (bi, 0, h))
    return pl.pallas_call(
        functools.partial(_moba_prompt_kernel, scale=dh ** -0.5),
        grid=(b, heads),
        in_specs=[spec, spec, spec],
        out_specs=spec,
        out_shape=jax.ShapeDtypeStruct((b, t, hd), BF16),
        compiler_params=_params("parallel", "parallel"),
        name="moba_prompt",
    )(q, k, v)


def _kmean_kernel(pt_ref, *refs, ppb):
    k_refs, o_ref = refs[:ppb], refs[ppb]
    n = pl.program_id(1)
    acc = jnp.sum(k_refs[0][0], axis=0, keepdims=True)
    for r in k_refs[1:]:
        acc = acc + jnp.sum(r[0], axis=0, keepdims=True)
    rows = ppb * k_refs[0].shape[1]
    o_ref[0, pl.ds(n, 1), :] = acc * (1.0 / rows)


def _kmean(page_table_flat, cache_k3, *, batch, n_pages, ppb):
    _, page, hd = cache_k3.shape
    nbp = n_pages // ppb
    specs = [
        pl.BlockSpec((1, page, hd), lambda b, n, pt, p=p: (pt[b * n_pages + n * ppb + p], 0, 0))
        for p in range(ppb)
    ]
    return pl.pallas_call(
        functools.partial(_kmean_kernel, ppb=ppb),
        grid_spec=pltpu.PrefetchScalarGridSpec(
            num_scalar_prefetch=1,
            grid=(batch, nbp),
            in_specs=specs,
            out_specs=pl.BlockSpec((1, nbp, hd), lambda b, n, pt: (b, 0, 0)),
        ),
        out_shape=jax.ShapeDtypeStruct((batch, nbp, hd), F32),
        compiler_params=_params("parallel", "arbitrary"),
        name="kmean",
    )(page_table_flat, *([cache_k3] * ppb))


def _gate_topk_kernel(q_ref, km_ref, idx_ref, ok_ref, *, dh):
    nbt, hd = q_ref.shape
    nbp = km_ref.shape[1]
    seg = (lax.broadcasted_iota(jnp.int32, (hd, LANES), 0) // dh
           == lax.broadcasted_iota(jnp.int32, (hd, LANES), 1)).astype(F32)
    n_iota = lax.broadcasted_iota(jnp.int32, (nbp, LANES), 0)
    idx_ref[...] = jnp.zeros(idx_ref.shape, jnp.int32)
    ok_ref[...] = jnp.zeros(ok_ref.shape, jnp.int32)
    for b in range(nbt):
        prod = km_ref[b] * q_ref[b:b + 1, :]
        g = jnp.dot(prod, seg, precision=lax.Precision.HIGHEST,
                    preferred_element_type=F32)
        for s in range(MOBA_TOPK):
            mx = jnp.max(g, axis=0, keepdims=True)
            idx = jnp.min(jnp.where(g == mx, n_iota, nbp), axis=0, keepdims=True)
            idx = jnp.minimum(idx, nbp - 1)
            idx_ref[b, s:s + 1, :] = idx
            ok_ref[b, s:s + 1, :] = (jnp.abs(mx) < INF).astype(jnp.int32)
            g = jnp.where(n_iota == idx, NEG_INF, g)


def _gate_topk(q, kmean, *, dh):
    b, hd = q.shape
    nbp = kmean.shape[1]
    out = jax.ShapeDtypeStruct((b, SUBLANES, LANES), jnp.int32)
    return pl.pallas_call(
        functools.partial(_gate_topk_kernel, dh=dh),
        grid=(1,),
        in_specs=[pl.BlockSpec((b, hd), lambda i: (0, 0)),
                  pl.BlockSpec((b, nbp, hd), lambda i: (0, 0, 0))],
        out_specs=[pl.BlockSpec((b, SUBLANES, LANES), lambda i: (0, 0, 0))] * 2,
        out_shape=[out, out],
        compiler_params=_params("arbitrary"),
        name="gate_topk",
    )(q, kmean)


def _moba_sample_kernel(pt_ref, idx_ref, ok_ref, q_ref, kn_ref, vn_ref, *refs, heads, ppb, scale):
    nsp = MOBA_TOPK * ppb
    k_refs, v_refs, o_ref = refs[:nsp], refs[nsp:2 * nsp], refs[2 * nsp]
    b = pl.program_id(0)
    h = pl.program_id(1)
    q = q_ref[0]
    q8 = jnp.broadcast_to(q, (SUBLANES, q.shape[1]))
    kk = jnp.concatenate([r[0] for r in k_refs], axis=0)
    vv = jnp.concatenate([r[0] for r in v_refs], axis=0)
    s = lax.dot_general(q8, kk, _NT, precision=lax.Precision.HIGHEST,
                        preferred_element_type=F32) * scale
    col = lax.broadcasted_iota(jnp.int32, s.shape, 1)
    for t in range(MOBA_TOPK):
        ok = ok_ref[(b * MOBA_TOPK + t) * heads + h]
        drop = (col // MOBA_BLOCK == t) & (ok == 0)
        s = jnp.where(drop, NEG_INF, s)
    s_own = jnp.sum(q * kn_ref[0], axis=1, keepdims=True) * scale
    mx = jnp.maximum(jnp.max(s, axis=1, keepdims=True), s_own)
    p = jnp.exp(s - mx)
    p_own = jnp.exp(s_own - mx)
    l = jnp.sum(p, axis=1, keepdims=True) + p_own
    o = jnp.dot(p, vv, precision=lax.Precision.HIGHEST, preferred_element_type=F32)
    o = (o + p_own * vn_ref[0]) / l
    o_ref[0] = o[0:1]


def _moba_sample(page_table_flat, idx_flat, ok_flat, q3, kn3, vn3, cache_k3, cache_v3, *,
                 heads, n_pages, ppb):
    b, _, hd = q3.shape
    dh = hd // heads
    page = cache_k3.shape[1]
    row_spec = pl.BlockSpec((1, 1, dh), lambda bi, h, pt, ix, ok: (bi, 0, h))

    def page_spec(s, p):
        def index_map(bi, h, pt, ix, ok):
            blk = ix[(bi * MOBA_TOPK + s) * heads + h]
            return (pt[bi * n_pages + blk * ppb + p], 0, h)
        return pl.BlockSpec((1, page, dh), index_map)

    page_specs = [page_spec(s, p) for s in range(MOBA_TOPK) for p in range(ppb)]
    return pl.pallas_call(
        functools.partial(_moba_sample_kernel, heads=heads, ppb=ppb, scale=dh ** -0.5),
        grid_spec=pltpu.PrefetchScalarGridSpec(
            num_scalar_prefetch=3,
            grid=(b, heads),
            in_specs=[row_spec, row_spec, row_spec] + page_specs + page_specs,
            out_specs=row_spec,
        ),
        out_shape=jax.ShapeDtypeStruct((b, 1, hd), F32),
        compiler_params=_params("parallel", "parallel"),
        name="moba_sample",
    )(page_table_flat, idx_flat, ok_flat, q3, kn3, vn3,
      *([cache_k3] * len(page_specs)), *([cache_v3] * len(page_specs)))


def kernel(x_prompt, x_sample, cache_k, cache_v, state_pool, page_table, g_pool, w_pool, s_pool,
           g_mlp, w_up, w_down, g_kv, w_k, w_v, g_attn, w_q, w_o, g_final):
    b, t, d = x_prompt.shape
    bs, ts, _ = x_sample.shape
    n_phys, page, heads, dh = cache_k.shape
    hd = heads * dh
    depth = w_up.shape[0]
    n_pool = g_pool.shape[0]
    n_pages = page_table.shape[1]
    past_len = n_pages * page
    assert ts == 1 and MOBA_BLOCK % page == 0 and past_len % MOBA_BLOCK == 0
    assert t % MOBA_BLOCK == 0 and t >= POOL_STATE and dh == LANES
    assert past_len // MOBA_BLOCK >= MOBA_TOPK and heads <= LANES
    ppb = MOBA_BLOCK // page

    w_pool_b, w_up_b, w_down_b = w_pool.astype(BF16), w_up.astype(BF16), w_down.astype(BF16)
    w_k_b, w_v_b, w_q_b, w_o_b = (w.astype(BF16) for w in (w_k, w_v, w_q, w_o))
    row = lambda a: a.reshape(1, -1)

    xp = x_prompt
    pool_p = []
    kp = vp = None
    for l in range(depth):
        if l < n_pool:
            xp, st = _pool_prompt(xp, row(g_pool[l]), w_pool_b, l, row(s_pool[l]), tp=512)
            pool_p.append(st)
            xf = xp.reshape(b * t, d)
        else:
            j = l - n_pool
            (q,) = _norm_proj(xf, row(g_attn[j]), [(w_q_b, j)], tm=512, tn=512)
            a = _moba_prompt(q.reshape(b, t, hd), kp, vp, heads=heads)
            xf = _out_proj(a.reshape(b * t, hd), w_o_b, j, xf, tm=1024, tn=512)
        last = l == depth - 1
        xf = _mlp(xf, row(g_mlp[l]), row(g_final), w_up_b, w_down_b, l,
                  final_norm=last, tm=512, tf=512)
        if l == n_pool - 1:
            kf, vf = _norm_proj(xf, row(g_kv), [(w_k_b, None), (w_v_b, None)], tm=512, tn=512)
            kp, vp = kf.reshape(b, t, hd), vf.reshape(b, t, hd)
        xp = xf.reshape(b, t, d)
    y_prompt = xp

    pt_flat = page_table.reshape(-1).astype(jnp.int32)
    cache_k3 = cache_k.reshape(n_phys, page, hd)
    cache_v3 = cache_v.reshape(n_phys, page, hd)
    kmean = _kmean(pt_flat, cache_k3, batch=bs, n_pages=n_pages, ppb=ppb)
    xs = x_sample.reshape(bs, d)
    pool_s = []
    ks = vs = None
    for l in range(depth):
        if l < n_pool:
            xs, st = _pool_sample(xs, state_pool, row(g_pool[l]), w_pool_b, l, row(s_pool[l]),
                                  pos0=past_len)
            pool_s.append(st)
        else:
            j = l - n_pool
            (q,) = _norm_proj(xs, row(g_attn[j]), [(w_q_b, j)], tm=bs, tn=1024)
            idx, ok = _gate_topk(q, kmean, dh=dh)
            idx_flat = idx[:, :MOBA_TOPK, :heads].reshape(-1)
            ok_flat = ok[:, :MOBA_TOPK, :heads].reshape(-1)
            a = _moba_sample(pt_flat, idx_flat, ok_flat, q.reshape(bs, 1, hd),
                             ks.reshape(bs, 1, hd), vs.reshape(bs, 1, hd), cache_k3, cache_v3,
                             heads=heads, n_pages=n_pages, ppb=ppb)
            xs = _out_proj(a.reshape(bs, hd), w_o_b, j, xs, tm=bs, tn=1024)
        last = l == depth - 1
        xs = _mlp(xs, row(g_mlp[l]), row(g_final), w_up_b, w_down_b, l,
                  final_norm=last, tm=bs, tf=1024)
        if l == n_pool - 1:
            ks, vs = _norm_proj(xs, row(g_kv), [(w_k_b, None), (w_v_b, None)], tm=bs, tn=1024)
    y_sample = xs.reshape(bs, ts, d)

    return (y_prompt, y_sample,
            kp.reshape(b, t, heads, dh), vp.reshape(b, t, heads, dh), jnp.stack(pool_p, axis=0),
            ks.reshape(bs, ts, heads, dh), vs.reshape(bs, ts, heads, dh), jnp.stack(pool_s, axis=0))
```

```python
import functools

import jax
import jax.numpy as jnp
from jax import lax
from jax.experimental import pallas as pl
from jax.experimental.pallas import tpu as pltpu

EPS = 1e-6
POOL_WINDOWS = (2, 4, 8, 16)
POOL_STATE = max(POOL_WINDOWS) - 1
HALO = max(POOL_WINDOWS)
MOBA_BLOCK = 256
MOBA_TOPK = 3

BF16 = jnp.bfloat16
F32 = jnp.float32
NEG_INF = float("-inf")
INF = float("inf")
MASK_BIAS = 2.0 ** 100
LOG2E = 1.4426950408889634
LANES = 128
SUBLANES = 8
VMEM_LIMIT = 60 * 1024 * 1024

_NT = (((1,), (1,)), ((), ()))


def _params(*semantics):
    return pltpu.CompilerParams(dimension_semantics=semantics, vmem_limit_bytes=VMEM_LIMIT)


def _rms(x, g):
    ms = jnp.mean(x * x, axis=-1, keepdims=True)
    return x * lax.rsqrt(ms + EPS) * g


def _mlp_kernel(x_ref, g_ref, gf_ref, wu_ref, wd_ref, *refs, final_norm, cast_next):
    if cast_next:
        wun_ref, wdn_ref, o_ref, wun_out, wdn_out, h_ref = refs
        wun_out[...] = wun_ref[...].astype(BF16)
        wdn_out[...] = wdn_ref[...].astype(BF16)
    else:
        o_ref, h_ref = refs
    j = pl.program_id(1)

    @pl.when(j == 0)
    def _():
        x = x_ref[...]
        h_ref[...] = _rms(x, g_ref[...]).astype(BF16)
        o_ref[...] = x

    u = jnp.dot(h_ref[...], wu_ref[...], preferred_element_type=F32)
    u = jnp.maximum(u, 0.0)
    u = (u * u).astype(BF16)
    o_ref[...] += jnp.dot(u, wd_ref[...], preferred_element_type=F32)

    if final_norm:
        @pl.when(j == pl.num_programs(1) - 1)
        def _():
            o_ref[...] = _rms(o_ref[...], gf_ref[...])


BF16_ROWS = 16


def _slab_plan(rows, cols, steps):
    best = None
    for cs in (1, 2, 4, 8):
        if cols % (cs * LANES):
            continue
        for rb in range(BF16_ROWS, rows + 1, BF16_ROWS):
            if rows % rb == 0 and (rows // rb) * cs <= steps:
                if best is None or rb * (cols // cs) < best[0] * (cols // best[1]):
                    best = (rb, cs)
                break
    assert best is not None, (rows, cols, steps)
    return best


def _slab_spec(rows, cols, steps, nj, layer):
    rb, cs = _slab_plan(rows, cols, steps)
    last = (rows // rb) * cs - 1

    def index(i, j):
        s = jnp.minimum(i * nj + j, last)
        return s // cs, s % cs

    spec_in = pl.BlockSpec((None, rb, cols // cs), lambda i, j: (layer,) + index(i, j))
    spec_out = pl.BlockSpec((rb, cols // cs), index)
    return spec_in, spec_out


def _mlp(x, g, gf, w_up, w_down, *, final_norm, tm, tf, cast_next=None):
    m, d = x.shape
    f = w_up.shape[1]
    tm = min(tm, m)
    tf = min(tf, f)
    ni, nj = m // tm, f // tf
    in_specs = [
        pl.BlockSpec((tm, d), lambda i, j: (i, 0), pipeline_mode=pl.Buffered(1)),
        pl.BlockSpec((1, d), lambda i, j: (0, 0)),
        pl.BlockSpec((1, d), lambda i, j: (0, 0)),
        pl.BlockSpec((d, tf), lambda i, j: (0, j)),
        pl.BlockSpec((tf, d), lambda i, j: (j, 0)),
    ]
    out_specs = [pl.BlockSpec((tm, d), lambda i, j: (i, 0))]
    out_shape = [jax.ShapeDtypeStruct((m, d), F32)]
    args = [x, g, gf, w_up, w_down]
    if cast_next is not None:
        wu_all, wd_all, layer = cast_next
        up_in, up_out = _slab_spec(d, f, ni * nj, nj, layer)
        dn_in, dn_out = _slab_spec(f, d, ni * nj, nj, layer)
        in_specs += [up_in, dn_in]
        out_specs += [up_out, dn_out]
        out_shape += [jax.ShapeDtypeStruct((d, f), BF16), jax.ShapeDtypeStruct((f, d), BF16)]
        args += [wu_all, wd_all]
    outs = pl.pallas_call(
        functools.partial(_mlp_kernel, final_norm=final_norm, cast_next=cast_next is not None),
        grid=(ni, nj),
        in_specs=in_specs,
        out_specs=out_specs,
        out_shape=out_shape,
        scratch_shapes=[pltpu.VMEM((tm, d), BF16)],
        compiler_params=_params("arbitrary", "arbitrary"),
        name="mlp",
    )(*args)
    return outs if cast_next is not None else outs[0]


def _norm_proj_kernel(x_ref, g_ref, *refs, n):
    w_refs, o_refs, h_ref = refs[:n], refs[n:2 * n], refs[2 * n]

    @pl.when(pl.program_id(1) == 0)
    def _():
        h_ref[...] = _rms(x_ref[...], g_ref[...]).astype(BF16)

    h = h_ref[...]
    for w_ref, o_ref in zip(w_refs, o_refs):
        o_ref[...] = jnp.dot(h, w_ref[...], preferred_element_type=F32)


def _norm_proj(x, g, weights, *, tm, tn):
    m, d = x.shape
    nout = weights[0][0].shape[-1]
    tm = min(tm, m)
    tn = min(tn, nout)
    w_specs = []
    for w, idx in weights:
        if idx is None:
            w_specs.append(pl.BlockSpec((d, tn), lambda i, j: (0, j)))
        else:
            w_specs.append(pl.BlockSpec((None, d, tn), lambda i, j, idx=idx: (idx, 0, j)))
    n = len(weights)
    return pl.pallas_call(
        functools.partial(_norm_proj_kernel, n=n),
        grid=(m // tm, nout // tn),
        in_specs=[pl.BlockSpec((tm, d), lambda i, j: (i, 0)),
                  pl.BlockSpec((1, d), lambda i, j: (0, 0))] + w_specs,
        out_specs=[pl.BlockSpec((tm, tn), lambda i, j: (i, j))] * n,
        out_shape=[jax.ShapeDtypeStruct((m, nout), F32)] * n,
        scratch_shapes=[pltpu.VMEM((tm, d), BF16)],
        compiler_params=_params("parallel", "arbitrary"),
        name="norm_proj",
    )(x, g, *[w for w, _ in weights])


def _out_proj_kernel(a_ref, w_ref, x_ref, o_ref):
    a = a_ref[...].astype(BF16)
    o_ref[...] = x_ref[...] + jnp.dot(a, w_ref[...], preferred_element_type=F32)


def _out_proj(a, w_o, layer, x, *, tm, tn):
    m, d = x.shape
    k = a.shape[1]
    tm = min(tm, m)
    tn = min(tn, d)
    return pl.pallas_call(
        _out_proj_kernel,
        grid=(m // tm, d // tn),
        in_specs=[
            pl.BlockSpec((tm, k), lambda i, j: (i, 0)),
            pl.BlockSpec((None, k, tn), lambda i, j: (layer, 0, j)),
            pl.BlockSpec((tm, tn), lambda i, j: (i, j)),
        ],
        out_specs=pl.BlockSpec((tm, tn), lambda i, j: (i, j)),
        out_shape=jax.ShapeDtypeStruct((m, d), F32),
        compiler_params=_params("parallel", "arbitrary"),
        name="out_proj",
    )(a, w_o, x)


def _window_sums(ext, w, rows):
    s, span = ext, 1
    while span < w:
        s = s[span:] + s[:-span]
        span *= 2
    start = HALO + 1 - w
    return s[start:start + rows]


def _pool_prompt_kernel(x_ref, halo_ref, g_ref, w_ref, s_ref, o_ref, st_ref, *, tp):
    i = pl.program_id(1)
    x = x_ref[0]
    g = g_ref[...]
    h = _rms(x, g)
    hh = _rms(halo_ref[0], g)
    hh = jnp.where(i > 0, hh, 0.0)
    ext = jnp.concatenate([hh, h], axis=0)
    cg = x.shape[1] // len(POOL_WINDOWS)
    pos = i * tp + lax.broadcasted_iota(jnp.int32, (tp, 1), 0)
    for gi, w in enumerate(POOL_WINDOWS):
        c0, c1 = gi * cg, (gi + 1) * cg
        wsum = _window_sums(ext[:, c0:c1], w, tp)
        cnt = jnp.minimum(pos + 1, w).astype(F32)
        dlt = wsum * (1.0 / cnt) - h[:, c0:c1]
        y = jnp.dot(dlt.astype(BF16), w_ref[gi], preferred_element_type=F32)
        o_ref[0, :, c0:c1] = x[:, c0:c1] + y * s_ref[:, c0:c1]

    @pl.when(i == pl.num_programs(1) - 1)
    def _():
        st_ref[0] = h[tp - POOL_STATE:, :]


def _pool_prompt(x, g, w_pool, layer, s, *, tp):
    b, t, d = x.shape
    tp = min(tp, t)
    cg = d // len(POOL_WINDOWS)
    hb = tp // HALO
    return pl.pallas_call(
        functools.partial(_pool_prompt_kernel, tp=tp),
        grid=(b, t // tp),
        in_specs=[
            pl.BlockSpec((1, tp, d), lambda bi, i: (bi, i, 0)),
            pl.BlockSpec((1, HALO, d), lambda bi, i: (bi, jnp.maximum(i * hb - 1, 0), 0)),
            pl.BlockSpec((1, d), lambda bi, i: (0, 0)),
            pl.BlockSpec((None, len(POOL_WINDOWS), cg, cg), lambda bi, i: (layer, 0, 0, 0)),
            pl.BlockSpec((1, d), lambda bi, i: (0, 0)),
        ],
        out_specs=[
            pl.BlockSpec((1, tp, d), lambda bi, i: (bi, i, 0)),
            pl.BlockSpec((1, POOL_STATE, d), lambda bi, i: (bi, 0, 0)),
        ],
        out_shape=[jax.ShapeDtypeStruct((b, t, d), F32),
                   jax.ShapeDtypeStruct((b, POOL_STATE, d), F32)],
        compiler_params=_params("parallel", "arbitrary"),
        name="pool_prompt",
    )(x, x, g, w_pool, s)


def _pool_sample_kernel(x_ref, prev_ref, g_ref, w_ref, s_ref, o_ref, st_ref, d_ref, *, pos0):
    x = x_ref[...]
    h = _rms(x, g_ref[...])
    nb = x.shape[0]
    cg = x.shape[1] // len(POOL_WINDOWS)
    for b in range(nb):
        pv = prev_ref[b]
        hb = h[b:b + 1]
        st_ref[b, 0:POOL_STATE - 1, :] = pv[1:POOL_STATE]
        st_ref[b, POOL_STATE - 1:POOL_STATE, :] = hb
        for gi, w in enumerate(POOL_WINDOWS):
            c0, c1 = gi * cg, (gi + 1) * cg
            wsum = hb[:, c0:c1] + jnp.sum(pv[POOL_STATE + 1 - w:, c0:c1], axis=0, keepdims=True)
            cnt = float(min(pos0 + 1, w))
            d_ref[b:b + 1, c0:c1] = wsum / cnt - hb[:, c0:c1]
    for gi in range(len(POOL_WINDOWS)):
        c0, c1 = gi * cg, (gi + 1) * cg
        y = jnp.dot(d_ref[:, c0:c1].astype(BF16), w_ref[gi], preferred_element_type=F32)
        o_ref[:, c0:c1] = x[:, c0:c1] + y * s_ref[:, c0:c1]


def _pool_sample(x, prev, g, w_pool, layer, s, *, pos0):
    b, d = x.shape
    cg = d // len(POOL_WINDOWS)
    return pl.pallas_call(
        functools.partial(_pool_sample_kernel, pos0=pos0),
        grid=(1,),
        in_specs=[
            pl.BlockSpec((b, d), lambda i: (0, 0)),
            pl.BlockSpec((None, b, POOL_STATE, d), lambda i: (layer, 0, 0, 0)),
            pl.BlockSpec((1, d), lambda i: (0, 0)),
            pl.BlockSpec((None, len(POOL_WINDOWS), cg, cg), lambda i: (layer, 0, 0, 0)),
            pl.BlockSpec((1, d), lambda i: (0, 0)),
        ],
        out_specs=[
            pl.BlockSpec((b, d), lambda i: (0, 0)),
            pl.BlockSpec((b, POOL_STATE, d), lambda i: (0, 0, 0)),
        ],
        out_shape=[jax.ShapeDtypeStruct((b, d), F32),
                   jax.ShapeDtypeStruct((b, POOL_STATE, d), F32)],
        scratch_shapes=[pltpu.VMEM((b, d), F32)],
        compiler_params=_params("arbitrary"),
        name="pool_sample",
    )(x, prev, g, w_pool, s)


ROW_CHUNK = 64


def _moba_prompt_kernel(q_ref, k_ref, v_ref, o_ref, qa_ref, ka_ref, va_ref, s_ref, p_ref, *, scale):
    t, dh = q_ref.shape[1], q_ref.shape[2]
    blk = MOBA_BLOCK
    nb = t // blk
    nbp = -(-nb // SUBLANES) * SUBLANES
    cexp = scale * LOG2E

    k = k_ref[0]
    key_blk = lax.broadcasted_iota(jnp.int32, (t, LANES), 0) // blk
    lane_t = lax.broadcasted_iota(jnp.int32, (t, LANES), 1)
    ka_ref[:, :dh] = k.astype(BF16)
    ka_ref[:, dh:] = jnp.where(key_blk == lane_t, -MASK_BIAS, 0.0).astype(BF16)
    va_ref[:, :dh] = v_ref[0].astype(BF16)
    va_ref[:, dh:] = jnp.where(lane_t == 0, 1.0, 0.0).astype(BF16)
    means = [jnp.mean(k[n * blk:(n + 1) * blk], axis=0, keepdims=True) for n in range(nb)]
    if nbp > nb:
        means.append(jnp.zeros((nbp - nb, dh), F32))
    kmean = jnp.concatenate(means, axis=0)

    km_hi = kmean.astype(BF16)
    km_lo = kmean - km_hi.astype(F32)
    km_parts = jnp.concatenate([km_hi.astype(F32), km_lo], axis=0).astype(BF16)

    n_iota = lax.broadcasted_iota(jnp.int32, (nbp, blk), 0)
    lane_q = lax.broadcasted_iota(jnp.int32, (blk, LANES), 1)
    row_c = lax.broadcasted_iota(jnp.int32, (ROW_CHUNK, blk), 0)
    col_c = lax.broadcasted_iota(jnp.int32, (ROW_CHUNK, blk), 1)

    def scores(c):
        buf = c % 2
        qc = q_ref[0, c * blk:(c + 1) * blk, :]
        q_hi = qc.astype(BF16)
        qa_ref[buf, :, :dh] = q_hi
        if c > 0:
            q_lo = (qc - q_hi.astype(F32)).astype(BF16)
            g_hi = lax.dot_general(km_parts, q_hi, _NT, preferred_element_type=F32)
            g_lo = lax.dot_general(km_parts, q_lo, _NT, preferred_element_type=F32)
            gt = g_hi[:nbp] + g_hi[nbp:] + g_lo[:nbp]
            past = n_iota < c
            gt = jnp.where(past, gt, NEG_INF)
            rank = jnp.zeros((nbp, blk), jnp.int32)
            for m in range(c):
                rm = gt[m:m + 1, :]
                beats = (rm > gt) | ((rm == gt) & (n_iota > m))
                rank = rank + beats.astype(jnp.int32)
            sel_t = past & (rank < MOBA_TOPK) & (jnp.abs(gt) < INF)
            sel_t = jnp.concatenate(
                [sel_t.astype(F32), jnp.zeros((LANES - nbp, blk), F32)], axis=0)
            sel = sel_t.T
            qa_ref[buf, :, dh:] = jnp.where(lane_q < c, 1.0 - sel, 0.0).astype(BF16)
        else:
            qa_ref[buf, :, dh:] = jnp.zeros((blk, LANES), BF16)
        nk = (c + 1) * blk
        s_ref[buf, :, :nk] = lax.dot_general(qa_ref[buf], ka_ref[:nk, :], _NT,
                                             preferred_element_type=F32)

    scores(0)
    for c in range(nb):
        buf = c % 2
        nk = (c + 1) * blk
        if c + 1 < nb:
            scores(c + 1)
        for r in range(blk // ROW_CHUNK):
            r0 = r * ROW_CHUNK
            causal = col_c <= row_c + r0
            sd = jnp.where(causal, s_ref[buf, r0:r0 + ROW_CHUNK, c * blk:nk], -MASK_BIAS)
            mx = sd
            for n in range(c):
                mx = jnp.maximum(mx, s_ref[buf, r0:r0 + ROW_CHUNK, n * blk:(n + 1) * blk])
            mb = jnp.broadcast_to(jnp.max(mx, axis=1, keepdims=True), (ROW_CHUNK, blk))
            p_ref[buf, r0:r0 + ROW_CHUNK, c * blk:nk] = jnp.exp2((sd - mb) * cexp).astype(BF16)
            for n in range(c):
                sn = s_ref[buf, r0:r0 + ROW_CHUNK, n * blk:(n + 1) * blk]
                p_ref[buf, r0:r0 + ROW_CHUNK, n * blk:(n + 1) * blk] = (
                    jnp.exp2((sn - mb) * cexp).astype(BF16))
        o = jnp.dot(p_ref[buf, :, :nk], va_ref[:nk, :], preferred_element_type=F32)
        o_ref[0, c * blk:(c + 1) * blk, :] = (o[:, :dh] * (1.0 / o[:, dh:dh + 1])).astype(o_ref.dtype)


def _moba_prompt(q, k, v, *, heads):
    b, t, hd = q.shape
    dh = hd // heads
    spec = pl.BlockSpec((1, t, dh), lambda bi, h: (bi, 0, h))
    return pl.pallas_call(
        functools.partial(_moba_prompt_kernel, scale=dh ** -0.5),
        grid=(b, heads),
        in_specs=[spec, spec, spec],
        out_specs=spec,
        out_shape=jax.ShapeDtypeStruct((b, t, hd), BF16),
        scratch_shapes=[
            pltpu.VMEM((2, MOBA_BLOCK, dh + LANES), BF16),
            pltpu.VMEM((t, dh + LANES), BF16),
            pltpu.VMEM((t, dh + LANES), BF16),
            pltpu.VMEM((2, MOBA_BLOCK, t), F32),
            pltpu.VMEM((2, MOBA_BLOCK, t), BF16),
        ],
        compiler_params=_params("parallel", "parallel"),
        name="moba_prompt",
    )(q, k, v)


def _kmean_kernel(pt_ref, *refs, ppb):
    k_refs, o_ref = refs[:ppb], refs[ppb]
    n = pl.program_id(1)
    acc = jnp.sum(k_refs[0][0], axis=0)
    for r in k_refs[1:]:
        acc = acc + jnp.sum(r[0], axis=0)
    rows = ppb * k_refs[0].shape[1]
    o_ref[0, pl.ds(n, 1)] = (acc * (1.0 / rows))[None]


def _kmean(page_table_flat, cache_k, *, batch, n_pages, ppb):
    _, page, heads, dh = cache_k.shape
    nbp = n_pages // ppb
    specs = [
        pl.BlockSpec((1, page, heads, dh),
                     lambda b, n, pt, p=p: (pt[b * n_pages + n * ppb + p], 0, 0, 0))
        for p in range(ppb)
    ]
    return pl.pallas_call(
        functools.partial(_kmean_kernel, ppb=ppb),
        grid_spec=pltpu.PrefetchScalarGridSpec(
            num_scalar_prefetch=1,
            grid=(batch, nbp),
            in_specs=specs,
            out_specs=pl.BlockSpec((1, nbp, heads, dh), lambda b, n, pt: (b, 0, 0, 0)),
        ),
        out_shape=jax.ShapeDtypeStruct((batch, nbp, heads, dh), F32),
        compiler_params=_params("parallel", "arbitrary"),
        name="kmean",
    )(page_table_flat, *([cache_k] * ppb))


def _gate_topk_kernel(q_ref, km_ref, idx_ref, ok_ref):
    nbt = q_ref.shape[0]
    nbp, heads = km_ref.shape[1], km_ref.shape[2]
    n_iota = lax.broadcasted_iota(jnp.int32, (nbp, heads, 1), 0)
    for b in range(nbt):
        g = jnp.sum(km_ref[b] * q_ref[b][None], axis=-1, keepdims=True)
        for s in range(MOBA_TOPK):
            mx = jnp.max(g, axis=0)
            idx = jnp.min(jnp.where(g == mx[None], n_iota, nbp), axis=0)
            idx = jnp.minimum(idx, nbp - 1)
            idx_ref[b, s] = idx
            ok_ref[b, s] = (jnp.abs(mx) < INF).astype(jnp.int32)
            g = jnp.where(n_iota == idx[None], NEG_INF, g)


def _gate_topk(q3, kmean):
    b, heads, dh = q3.shape
    nbp = kmean.shape[1]
    out = jax.ShapeDtypeStruct((b, MOBA_TOPK, heads, 1), jnp.int32)
    out_spec = pl.BlockSpec((b, MOBA_TOPK, heads, 1), lambda i: (0, 0, 0, 0))
    return pl.pallas_call(
        _gate_topk_kernel,
        grid=(1,),
        in_specs=[pl.BlockSpec((b, heads, dh), lambda i: (0, 0, 0)),
                  pl.BlockSpec((b, nbp, heads, dh), lambda i: (0, 0, 0, 0))],
        out_specs=[out_spec, out_spec],
        out_shape=[out, out],
        compiler_params=_params("arbitrary"),
        name="gate_topk",
    )(q3, kmean)


def _moba_sample_kernel(pt_ref, idx_ref, ok_ref, q_ref, kn_ref, vn_ref, ck_ref, cv_ref, o_ref,
                        kbuf, vbuf, sem, *, heads, n_pages, ppb, scale):
    n_pairs = q_ref.shape[0]
    nsp, page, dh = kbuf.shape[1], kbuf.shape[2], kbuf.shape[3]
    rows = nsp * page

    def gather(pair, slot):
        b, h = pair // heads, pair % heads
        cps = []
        for s in range(MOBA_TOPK):
            blk = idx_ref[(b * MOBA_TOPK + s) * heads + h]
            for p in range(ppb):
                phys = pt_ref[b * n_pages + blk * ppb + p]
                j = s * ppb + p
                cps.append(pltpu.make_async_copy(ck_ref.at[phys, :, h, :], kbuf.at[slot, j], sem.at[0, slot]))
                cps.append(pltpu.make_async_copy(cv_ref.at[phys, :, h, :], vbuf.at[slot, j], sem.at[1, slot]))
        return cps

    for cp in gather(0, 0):
        cp.start()
    row_blk = lax.broadcasted_iota(jnp.int32, (rows, 1), 0) // MOBA_BLOCK

    def body(pair, carry):
        slot = pair % 2

        @pl.when(pair + 1 < n_pairs)
        def _():
            for cp in gather(pair + 1, 1 - slot):
                cp.start()

        for cp in gather(pair, slot):
            cp.wait()
        b, h = pair // heads, pair % heads
        kk = kbuf[slot].reshape(rows, dh)
        vv = vbuf[slot].reshape(rows, dh)
        q = q_ref[pl.ds(pair, 1), :]
        s = jnp.sum(kk * q, axis=1, keepdims=True) * scale
        for j in range(MOBA_TOPK):
            ok = ok_ref[(b * MOBA_TOPK + j) * heads + h]
            s = s + jnp.where(row_blk == j, jnp.where(ok == 0, NEG_INF, 0.0), 0.0)
        s_own = jnp.sum(q * kn_ref[pl.ds(pair, 1), :], axis=1, keepdims=True) * scale
        mx = jnp.maximum(jnp.max(s, axis=0, keepdims=True), s_own)
        p = jnp.exp(s - mx)
        p_own = jnp.exp(s_own - mx)
        l = jnp.sum(p, axis=0, keepdims=True) + p_own
        o = jnp.sum(p * vv, axis=0, keepdims=True) + p_own * vn_ref[pl.ds(pair, 1), :]
        o_ref[pl.ds(pair, 1), :] = o / l
        return carry

    lax.fori_loop(0, n_pairs, body, 0)


def _moba_sample(page_table_flat, idx_flat, ok_flat, q2, kn2, vn2, cache_k, cache_v, *,
                 heads, n_pages, ppb):
    n_pairs, dh = q2.shape
    page = cache_k.shape[1]
    nsp = MOBA_TOPK * ppb
    full = pl.BlockSpec((n_pairs, dh), lambda i, pt, ix, ok: (0, 0))
    hbm = pl.BlockSpec(memory_space=pl.ANY)
    return pl.pallas_call(
        functools.partial(_moba_sample_kernel, heads=heads, n_pages=n_pages, ppb=ppb,
                          scale=dh ** -0.5),
        grid_spec=pltpu.PrefetchScalarGridSpec(
            num_scalar_prefetch=3,
            grid=(1,),
            in_specs=[full, full, full, hbm, hbm],
            out_specs=full,
            scratch_shapes=[
                pltpu.VMEM((2, nsp, page, dh), F32),
                pltpu.VMEM((2, nsp, page, dh), F32),
                pltpu.SemaphoreType.DMA((2, 2)),
            ],
        ),
        out_shape=jax.ShapeDtypeStruct((n_pairs, dh), F32),
        compiler_params=_params("arbitrary"),
        name="moba_sample",
    )(page_table_flat, idx_flat, ok_flat, q2, kn2, vn2, cache_k, cache_v)


def kernel(x_prompt, x_sample, cache_k, cache_v, state_pool, page_table, g_pool, w_pool, s_pool,
           g_mlp, w_up, w_down, g_kv, w_k, w_v, g_attn, w_q, w_o, g_final):
    b, t, d = x_prompt.shape
    bs, ts, _ = x_sample.shape
    n_phys, page, heads, dh = cache_k.shape
    hd = heads * dh
    depth = w_up.shape[0]
    n_pool = g_pool.shape[0]
    n_pages = page_table.shape[1]
    past_len = n_pages * page
    assert ts == 1 and MOBA_BLOCK % page == 0 and past_len % MOBA_BLOCK == 0
    assert t % MOBA_BLOCK == 0 and t >= POOL_STATE and dh == LANES
    assert past_len // MOBA_BLOCK >= MOBA_TOPK and heads <= LANES
    ppb = MOBA_BLOCK // page

    w_pool_b = w_pool.astype(BF16)
    w_k_b, w_v_b, w_q_b, w_o_b = (w.astype(BF16) for w in (w_k, w_v, w_q, w_o))
    wu_b = [w_up[0].astype(BF16)] + [None] * (depth - 1)
    wd_b = [w_down[0].astype(BF16)] + [None] * (depth - 1)
    row = lambda a: a.reshape(1, -1)

    xp = x_prompt
    pool_p = []
    kp = vp = None
    for l in range(depth):
        if l < n_pool:
            xp, st = _pool_prompt(xp, row(g_pool[l]), w_pool_b, l, row(s_pool[l]), tp=512)
            pool_p.append(st)
            xf = xp.reshape(b * t, d)
        else:
            j = l - n_pool
            (q,) = _norm_proj(xf, row(g_attn[j]), [(w_q_b, j)], tm=512, tn=1024)
            a = _moba_prompt(q.reshape(b, t, hd), kp, vp, heads=heads)
            xf = _out_proj(a.reshape(b * t, hd), w_o_b, j, xf, tm=1024, tn=512)
        last = l == depth - 1
        if last:
            xf = _mlp(xf, row(g_mlp[l]), row(g_final), wu_b[l], wd_b[l],
                      final_norm=True, tm=512, tf=512)
        else:
            xf, wu_b[l + 1], wd_b[l + 1] = _mlp(
                xf, row(g_mlp[l]), row(g_final), wu_b[l], wd_b[l],
                final_norm=False, tm=512, tf=512, cast_next=(w_up, w_down, l + 1))
        if l == n_pool - 1:
            kf, vf = _norm_proj(xf, row(g_kv), [(w_k_b, None), (w_v_b, None)], tm=512, tn=512)
            kp, vp = kf.reshape(b, t, hd), vf.reshape(b, t, hd)
        xp = xf.reshape(b, t, d)
    y_prompt = xp

    pt_flat = page_table.reshape(-1).astype(jnp.int32)
    kmean = _kmean(pt_flat, cache_k, batch=bs, n_pages=n_pages, ppb=ppb)
    xs = x_sample.reshape(bs, d)
    pool_s = []
    ks = vs = None
    for l in range(depth):
        if l < n_pool:
            xs, st = _pool_sample(xs, state_pool, row(g_pool[l]), w_pool_b, l, row(s_pool[l]),
                                  pos0=past_len)
            pool_s.append(st)
        else:
            j = l - n_pool
            (q,) = _norm_proj(xs, row(g_attn[j]), [(w_q_b, j)], tm=bs, tn=1024)
            idx, ok = _gate_topk(q.reshape(bs, heads, dh), kmean)
            a = _moba_sample(pt_flat, idx.reshape(-1), ok.reshape(-1), q.reshape(bs * heads, dh),
                             ks.reshape(bs * heads, dh), vs.reshape(bs * heads, dh),
                             cache_k, cache_v, heads=heads, n_pages=n_pages, ppb=ppb)
            xs = _out_proj(a.reshape(bs, hd), w_o_b, j, xs, tm=bs, tn=1024)
        last = l == depth - 1
        xs = _mlp(xs, row(g_mlp[l]), row(g_final), wu_b[l], wd_b[l],
                  final_norm=last, tm=bs, tf=1024)
        if l == n_pool - 1:
            ks, vs = _norm_proj(xs, row(g_kv), [(w_k_b, None), (w_v_b, None)], tm=bs, tn=1024)
    y_sample = xs.reshape(bs, ts, d)

    return (y_prompt, y_sample,
            kp.reshape(b, t, heads, dh), vp.reshape(b, t, heads, dh), jnp.stack(pool_p, axis=0),
            ks.reshape(bs, ts, heads, dh), vs.reshape(bs, ts, heads, dh), jnp.stack(pool_s, axis=0))
```

```python
import functools

import jax
import jax.numpy as jnp
from jax import lax
from jax.experimental import pallas as pl
from jax.experimental.pallas import tpu as pltpu

EPS = 1e-6
POOL_WINDOWS = (2, 4, 8, 16)
POOL_STATE = max(POOL_WINDOWS) - 1
HALO = max(POOL_WINDOWS)
MOBA_BLOCK = 256
MOBA_TOPK = 3

BF16 = jnp.bfloat16
F32 = jnp.float32
NEG_INF = float("-inf")
INF = float("inf")
MASK_BIAS = 2.0 ** 100
LOG2E = 1.4426950408889634
LANES = 128
SUBLANES = 8
BF16_ROWS = 16
VMEM_LIMIT = 60 * 1024 * 1024

_NT = (((1,), (1,)), ((), ()))


def _params(*semantics):
    return pltpu.CompilerParams(dimension_semantics=semantics, vmem_limit_bytes=VMEM_LIMIT)


def _rms(x, g):
    ms = jnp.mean(x * x, axis=-1, keepdims=True)
    return x * lax.rsqrt(ms + EPS) * g


def _rms_rows(x_ref, g, store):
    n = x_ref.shape[0]
    step = min(n, BF16_ROWS)
    for r0 in range(0, n, step):
        parts = [_rms(x_ref[r:r + SUBLANES, :], g) for r in range(r0, r0 + step, SUBLANES)]
        store(slice(r0, r0 + step), parts[0] if len(parts) == 1 else jnp.concatenate(parts, axis=0))


def _mlp_kernel(x_ref, g_ref, gf_ref, wu_ref, wd_ref, *refs, final_norm, n_extra, n_cast):
    refs = list(refs)
    xe_ref = refs.pop(0) if n_extra else None
    cast_in = [refs.pop(0) for _ in range(n_cast)]
    o_ref = refs.pop(0)
    oe_ref = refs.pop(0) if n_extra else None
    cast_out = [refs.pop(0) for _ in range(n_cast)]
    (h_ref,) = refs
    i, j = pl.program_id(0), pl.program_id(1)
    tm = x_ref.shape[0]
    g = g_ref[...]

    @pl.when(j == 0)
    def _():
        def store(rows, h):
            h_ref[rows, :] = h.astype(BF16)
        _rms_rows(x_ref, g, store)
        o_ref[...] = x_ref[...]

    def step(rows):
        for src, dst in zip(cast_in, cast_out):
            dst[...] = src[...].astype(BF16)
        u = jnp.dot(h_ref[:rows, :], wu_ref[...], preferred_element_type=F32)
        u = jnp.maximum(u, 0.0)
        u = (u * u).astype(BF16)
        return jnp.dot(u, wd_ref[...], preferred_element_type=F32)

    if n_extra:
        @pl.when((i == 0) & (j == 0))
        def _():
            xe = xe_ref[...]
            pad = jnp.zeros((BF16_ROWS - n_extra, xe.shape[1]), F32)
            h_ref[tm:, :] = jnp.concatenate([_rms(xe, g), pad], axis=0).astype(BF16)
            oe_ref[...] = xe

        @pl.when(i == 0)
        def _():
            r = step(tm + BF16_ROWS)
            o_ref[...] += r[:tm]
            oe_ref[...] += r[tm:tm + n_extra]

        @pl.when(i > 0)
        def _():
            o_ref[...] += step(tm)
    else:
        o_ref[...] += step(tm)

    if final_norm:
        @pl.when(j == pl.num_programs(1) - 1)
        def _():
            def store(rows, y):
                o_ref[rows, :] = y
            _rms_rows(o_ref, gf_ref[...], store)
            if n_extra:
                @pl.when(i == 0)
                def _():
                    oe_ref[...] = _rms(oe_ref[...], gf_ref[...])


def _slab_plan(rows, cols, steps):
    best = None
    for cs in (1, 2, 4, 8):
        if cols % (cs * LANES):
            continue
        for rb in range(BF16_ROWS, rows + 1, BF16_ROWS):
            if rows % rb == 0 and (rows // rb) * cs <= steps:
                if best is None or rb * (cols // cs) < best[0] * (cols // best[1]):
                    best = (rb, cs)
                break
    assert best is not None, (rows, cols, steps)
    return best


def _slab_spec(rows, cols, steps, nj, lead):
    rb, cs = _slab_plan(rows, cols, steps)
    last = (rows // rb) * cs - 1

    def index(i, j):
        s = jnp.minimum(i * nj + j, last)
        return s // cs, s % cs

    if lead is None:
        spec_in = pl.BlockSpec((rb, cols // cs), index)
    else:
        spec_in = pl.BlockSpec((None, rb, cols // cs), lambda i, j: (lead,) + index(i, j))
    return spec_in, pl.BlockSpec((rb, cols // cs), index)


def _mlp(x, g, gf, w_up, w_down, *, final_norm, tm, tf, extra=None, casts=()):
    m, d = x.shape
    f = w_up.shape[1]
    tm = min(tm, m)
    tf = min(tf, f)
    ni, nj = m // tm, f // tf
    n_extra = 0 if extra is None else extra.shape[0]
    assert n_extra % SUBLANES == 0 and n_extra <= BF16_ROWS
    const = lambda i, j: (0, 0)
    in_specs = [
        pl.BlockSpec((tm, d), lambda i, j: (i, 0), pipeline_mode=pl.Buffered(1)),
        pl.BlockSpec((1, d), const),
        pl.BlockSpec((1, d), const),
        pl.BlockSpec((d, tf), lambda i, j: (0, j)),
        pl.BlockSpec((tf, d), lambda i, j: (j, 0)),
    ]
    out_specs = [pl.BlockSpec((tm, d), lambda i, j: (i, 0))]
    out_shape = [jax.ShapeDtypeStruct((m, d), F32)]
    args = [x, g, gf, w_up, w_down]
    if n_extra:
        in_specs.append(pl.BlockSpec((n_extra, d), const))
        out_specs.append(pl.BlockSpec((n_extra, d), const))
        out_shape.append(jax.ShapeDtypeStruct((n_extra, d), F32))
        args.append(extra)
    for arr, lead in casts:
        rows, cols = arr.shape[-2:]
        spec_in, spec_out = _slab_spec(rows, cols, ni * nj, nj, lead)
        in_specs.append(spec_in)
        out_specs.append(spec_out)
        out_shape.append(jax.ShapeDtypeStruct((rows, cols), BF16))
        args.append(arr)
    outs = pl.pallas_call(
        functools.partial(_mlp_kernel, final_norm=final_norm, n_extra=n_extra, n_cast=len(casts)),
        grid=(ni, nj),
        in_specs=in_specs,
        out_specs=out_specs,
        out_shape=out_shape,
        scratch_shapes=[pltpu.VMEM((tm + (BF16_ROWS if n_extra else 0), d), BF16)],
        compiler_params=_params("arbitrary", "arbitrary"),
        name="mlp",
    )(*args)
    k = 2 if n_extra else 1
    return outs[0], (outs[1] if n_extra else None), list(outs[k:])


def _norm_proj_kernel(x_ref, g_ref, *refs, n):
    w_refs, o_refs, h_ref = refs[:n], refs[n:2 * n], refs[2 * n]

    @pl.when(pl.program_id(1) == 0)
    def _():
        def store(rows, h):
            h_ref[rows, :] = h.astype(BF16)
        _rms_rows(x_ref, g_ref[...], store)

    h = h_ref[...]
    for w_ref, o_ref in zip(w_refs, o_refs):
        o_ref[...] = jnp.dot(h, w_ref[...], preferred_element_type=F32)


def _norm_proj(x, g, weights, *, tm, tn):
    m, d = x.shape
    nout = weights[0].shape[1]
    tm = min(tm, m)
    tn = min(tn, nout)
    n = len(weights)
    return pl.pallas_call(
        functools.partial(_norm_proj_kernel, n=n),
        grid=(m // tm, nout // tn),
        in_specs=[pl.BlockSpec((tm, d), lambda i, j: (i, 0)),
                  pl.BlockSpec((1, d), lambda i, j: (0, 0))]
                 + [pl.BlockSpec((d, tn), lambda i, j: (0, j))] * n,
        out_specs=[pl.BlockSpec((tm, tn), lambda i, j: (i, j))] * n,
        out_shape=[jax.ShapeDtypeStruct((m, nout), F32)] * n,
        scratch_shapes=[pltpu.VMEM((tm, d), BF16)],
        compiler_params=_params("parallel", "arbitrary"),
        name="norm_proj",
    )(x, g, *weights)


def _out_proj_kernel(a_ref, w_ref, x_ref, o_ref):
    a = a_ref[...].astype(BF16)
    o_ref[...] = x_ref[...] + jnp.dot(a, w_ref[...], preferred_element_type=F32)


def _out_proj(a, w_o, x, *, tm, tn):
    m, d = x.shape
    k = a.shape[1]
    tm = min(tm, m)
    tn = min(tn, d)
    return pl.pallas_call(
        _out_proj_kernel,
        grid=(m // tm, d // tn),
        in_specs=[
            pl.BlockSpec((tm, k), lambda i, j: (i, 0)),
            pl.BlockSpec((k, tn), lambda i, j: (0, j)),
            pl.BlockSpec((tm, tn), lambda i, j: (i, j)),
        ],
        out_specs=pl.BlockSpec((tm, tn), lambda i, j: (i, j)),
        out_shape=jax.ShapeDtypeStruct((m, d), F32),
        compiler_params=_params("parallel", "arbitrary"),
        name="out_proj",
    )(a, w_o, x)


def _window_sums(ext, w, rows):
    s, span = ext, 1
    while span < w:
        s = s[span:] + s[:-span]
        span *= 2
    start = HALO + 1 - w
    return s[start:start + rows]


def _pool_prompt_kernel(x_ref, halo_ref, g_ref, w_ref, s_ref, o_ref, st_ref, *, tp):
    i = pl.program_id(1)
    x = x_ref[0]
    g = g_ref[...]
    h = _rms(x, g)
    hh = _rms(halo_ref[0], g)
    hh = jnp.where(i > 0, hh, 0.0)
    ext = jnp.concatenate([hh, h], axis=0)
    cg = x.shape[1] // len(POOL_WINDOWS)
    pos = i * tp + lax.broadcasted_iota(jnp.int32, (tp, 1), 0)
    for gi, w in enumerate(POOL_WINDOWS):
        c0, c1 = gi * cg, (gi + 1) * cg
        wsum = _window_sums(ext[:, c0:c1], w, tp)
        cnt = jnp.minimum(pos + 1, w).astype(F32)
        dlt = wsum * (1.0 / cnt) - h[:, c0:c1]
        y = jnp.dot(dlt.astype(BF16), w_ref[gi], preferred_element_type=F32)
        o_ref[0, :, c0:c1] = x[:, c0:c1] + y * s_ref[:, c0:c1]

    @pl.when(i == pl.num_programs(1) - 1)
    def _():
        st_ref[0] = h[tp - POOL_STATE:, :]


def _pool_prompt(x, g, w_pool, layer, s, *, tp):
    b, t, d = x.shape
    tp = min(tp, t)
    cg = d // len(POOL_WINDOWS)
    hb = tp // HALO
    return pl.pallas_call(
        functools.partial(_pool_prompt_kernel, tp=tp),
        grid=(b, t // tp),
        in_specs=[
            pl.BlockSpec((1, tp, d), lambda bi, i: (bi, i, 0)),
            pl.BlockSpec((1, HALO, d), lambda bi, i: (bi, jnp.maximum(i * hb - 1, 0), 0)),
            pl.BlockSpec((1, d), lambda bi, i: (0, 0)),
            pl.BlockSpec((None, len(POOL_WINDOWS), cg, cg), lambda bi, i: (layer, 0, 0, 0)),
            pl.BlockSpec((1, d), lambda bi, i: (0, 0)),
        ],
        out_specs=[
            pl.BlockSpec((1, tp, d), lambda bi, i: (bi, i, 0)),
            pl.BlockSpec((1, POOL_STATE, d), lambda bi, i: (bi, 0, 0)),
        ],
        out_shape=[jax.ShapeDtypeStruct((b, t, d), F32),
                   jax.ShapeDtypeStruct((b, POOL_STATE, d), F32)],
        compiler_params=_params("parallel", "arbitrary"),
        name="pool_prompt",
    )(x, x, g, w_pool, s)


def _pool_sample_kernel(x_ref, prev_ref, g_ref, w_ref, s_ref, o_ref, st_ref, d_ref, *, pos0):
    x = x_ref[...]
    h = _rms(x, g_ref[...])
    nb = x.shape[0]
    cg = x.shape[1] // len(POOL_WINDOWS)
    for b in range(nb):
        pv = prev_ref[b]
        hb = h[b:b + 1]
        st_ref[b, 0:POOL_STATE - 1, :] = pv[1:POOL_STATE]
        st_ref[b, POOL_STATE - 1:POOL_STATE, :] = hb
        for gi, w in enumerate(POOL_WINDOWS):
            c0, c1 = gi * cg, (gi + 1) * cg
            wsum = hb[:, c0:c1] + jnp.sum(pv[POOL_STATE + 1 - w:, c0:c1], axis=0, keepdims=True)
            cnt = float(min(pos0 + 1, w))
            d_ref[b:b + 1, c0:c1] = wsum / cnt - hb[:, c0:c1]
    for gi in range(len(POOL_WINDOWS)):
        c0, c1 = gi * cg, (gi + 1) * cg
        y = jnp.dot(d_ref[:, c0:c1].astype(BF16), w_ref[gi], preferred_element_type=F32)
        o_ref[:, c0:c1] = x[:, c0:c1] + y * s_ref[:, c0:c1]


def _pool_sample(x, prev, g, w_pool, layer, s, *, pos0):
    b, d = x.shape
    cg = d // len(POOL_WINDOWS)
    return pl.pallas_call(
        functools.partial(_pool_sample_kernel, pos0=pos0),
        grid=(1,),
        in_specs=[
            pl.BlockSpec((b, d), lambda i: (0, 0)),
            pl.BlockSpec((None, b, POOL_STATE, d), lambda i: (layer, 0, 0, 0)),
            pl.BlockSpec((1, d), lambda i: (0, 0)),
            pl.BlockSpec((None, len(POOL_WINDOWS), cg, cg), lambda i: (layer, 0, 0, 0)),
            pl.BlockSpec((1, d), lambda i: (0, 0)),
        ],
        out_specs=[
            pl.BlockSpec((b, d), lambda i: (0, 0)),
            pl.BlockSpec((b, POOL_STATE, d), lambda i: (0, 0, 0)),
        ],
        out_shape=[jax.ShapeDtypeStruct((b, d), F32),
                   jax.ShapeDtypeStruct((b, POOL_STATE, d), F32)],
        scratch_shapes=[pltpu.VMEM((b, d), F32)],
        compiler_params=_params("arbitrary"),
        name="pool_sample",
    )(x, prev, g, w_pool, s)


ROW_CHUNK = 64


def _moba_prompt_kernel(q_ref, k_ref, v_ref, o_ref, qa_ref, ka_ref, va_ref, s_ref, p_ref, *, scale):
    t, dh = q_ref.shape[1], q_ref.shape[2]
    blk = MOBA_BLOCK
    nb = t // blk
    nbp = -(-nb // SUBLANES) * SUBLANES
    cexp = scale * LOG2E

    k = k_ref[0]
    lane_t = lax.broadcasted_iota(jnp.int32, (t, LANES), 1)
    ka_ref[:dh, :] = k.T.astype(BF16)
    mask_row = lax.broadcasted_iota(jnp.int32, (LANES, t), 0)
    mask_key = lax.broadcasted_iota(jnp.int32, (LANES, t), 1) // blk
    ka_ref[dh:, :] = jnp.where(mask_row == mask_key, -MASK_BIAS, 0.0).astype(BF16)
    va_ref[:, :dh] = v_ref[0].astype(BF16)
    va_ref[:, dh:] = jnp.where(lane_t == 0, 1.0, 0.0).astype(BF16)
    means = [jnp.mean(k[n * blk:(n + 1) * blk], axis=0, keepdims=True) for n in range(nb)]
    if nbp > nb:
        means.append(jnp.zeros((nbp - nb, dh), F32))
    kmean = jnp.concatenate(means, axis=0)

    km_hi = kmean.astype(BF16)
    km_lo = kmean - km_hi.astype(F32)
    km_parts = jnp.concatenate([km_hi.astype(F32), km_lo], axis=0).astype(BF16)

    n_iota = lax.broadcasted_iota(jnp.int32, (nbp, blk), 0)
    lane_q = lax.broadcasted_iota(jnp.int32, (blk, LANES), 1)
    row_c = lax.broadcasted_iota(jnp.int32, (ROW_CHUNK, blk), 0)
    col_c = lax.broadcasted_iota(jnp.int32, (ROW_CHUNK, blk), 1)

    def scores(c):
        buf = c % 2
        qc = q_ref[0, c * blk:(c + 1) * blk, :]
        q_hi = qc.astype(BF16)
        qa_ref[buf, :, :dh] = q_hi
        if c > 0:
            q_lo = (qc - q_hi.astype(F32)).astype(BF16)
            g_hi = lax.dot_general(km_parts, q_hi, _NT, preferred_element_type=F32)
            g_lo = lax.dot_general(km_parts, q_lo, _NT, preferred_element_type=F32)
            gt = g_hi[:nbp] + g_hi[nbp:] + g_lo[:nbp]
            past = n_iota < c
            gt = jnp.where(past, gt, NEG_INF)
            rank = jnp.zeros((nbp, blk), jnp.int32)
            for m in range(c):
                rm = gt[m:m + 1, :]
                beats = (rm > gt) | ((rm == gt) & (n_iota > m))
                rank = rank + beats.astype(jnp.int32)
            sel_t = past & (rank < MOBA_TOPK) & (jnp.abs(gt) < INF)
            sel_t = jnp.concatenate(
                [sel_t.astype(F32), jnp.zeros((LANES - nbp, blk), F32)], axis=0)
            sel = sel_t.T
            qa_ref[buf, :, dh:] = jnp.where(lane_q < c, 1.0 - sel, 0.0).astype(BF16)
        else:
            qa_ref[buf, :, dh:] = jnp.zeros((blk, LANES), BF16)
        nk = (c + 1) * blk
        s_ref[buf, :, :nk] = jnp.dot(qa_ref[buf], ka_ref[:, :nk], preferred_element_type=F32)

    scores(0)
    for c in range(nb):
        buf = c % 2
        nk = (c + 1) * blk
        if c + 1 < nb:
            scores(c + 1)
        for r in range(blk // ROW_CHUNK):
            r0 = r * ROW_CHUNK
            causal = col_c <= row_c + r0
            sd = jnp.where(causal, s_ref[buf, r0:r0 + ROW_CHUNK, c * blk:nk], -MASK_BIAS)
            mx = sd
            for n in range(c):
                mx = jnp.maximum(mx, s_ref[buf, r0:r0 + ROW_CHUNK, n * blk:(n + 1) * blk])
            mb = jnp.broadcast_to(jnp.max(mx, axis=1, keepdims=True), (ROW_CHUNK, blk))
            p_ref[buf, r0:r0 + ROW_CHUNK, c * blk:nk] = jnp.exp2((sd - mb) * cexp).astype(BF16)
            for n in range(c):
                sn = s_ref[buf, r0:r0 + ROW_CHUNK, n * blk:(n + 1) * blk]
                p_ref[buf, r0:r0 + ROW_CHUNK, n * blk:(n + 1) * blk] = (
                    jnp.exp2((sn - mb) * cexp).astype(BF16))
        o = jnp.dot(p_ref[buf, :, :nk], va_ref[:nk, :], preferred_element_type=F32)
        o_ref[0, c * blk:(c + 1) * blk, :] = (o[:, :dh] * (1.0 / o[:, dh:dh + 1])).astype(o_ref.dtype)


def _moba_prompt(q, k, v, *, heads):
    b, t, hd = q.shape
    dh = hd // heads
    spec = pl.BlockSpec((1, t, dh), lambda bi, h: (bi, 0, h))
    return pl.pallas_call(
        functools.partial(_moba_prompt_kernel, scale=dh ** -0.5),
        grid=(b, heads),
        in_specs=[spec, spec, spec],
        out_specs=spec,
        out_shape=jax.ShapeDtypeStruct((b, t, hd), BF16),
        scratch_shapes=[
            pltpu.VMEM((2, MOBA_BLOCK, dh + LANES), BF16),
            pltpu.VMEM((dh + LANES, t), BF16),
            pltpu.VMEM((t, dh + LANES), BF16),
            pltpu.VMEM((2, MOBA_BLOCK, t), F32),
            pltpu.VMEM((2, MOBA_BLOCK, t), BF16),
        ],
        compiler_params=_params("parallel", "parallel"),
        name="moba_prompt",
    )(q, k, v)


def _kmean_kernel(pt_ref, *refs, ppb):
    k_refs, o_ref = refs[:ppb], refs[ppb]
    n = pl.program_id(1)
    acc = jnp.sum(k_refs[0][0], axis=0)
    for r in k_refs[1:]:
        acc = acc + jnp.sum(r[0], axis=0)
    rows = ppb * k_refs[0].shape[1]
    o_ref[0, pl.ds(n, 1)] = (acc * (1.0 / rows))[None]


def _kmean(page_table_flat, cache_k, *, batch, n_pages, ppb):
    _, page, heads, dh = cache_k.shape
    nbp = n_pages // ppb
    specs = [
        pl.BlockSpec((1, page, heads, dh),
                     lambda b, n, pt, p=p: (pt[b * n_pages + n * ppb + p], 0, 0, 0))
        for p in range(ppb)
    ]
    return pl.pallas_call(
        functools.partial(_kmean_kernel, ppb=ppb),
        grid_spec=pltpu.PrefetchScalarGridSpec(
            num_scalar_prefetch=1,
            grid=(batch, nbp),
            in_specs=specs,
            out_specs=pl.BlockSpec((1, nbp, heads, dh), lambda b, n, pt: (b, 0, 0, 0)),
        ),
        out_shape=jax.ShapeDtypeStruct((batch, nbp, heads, dh), F32),
        compiler_params=_params("parallel", "arbitrary"),
        name="kmean",
    )(page_table_flat, *([cache_k] * ppb))


def _gate_topk_kernel(q_ref, km_ref, idx_ref, ok_ref):
    nbt = q_ref.shape[0]
    nbp, heads = km_ref.shape[1], km_ref.shape[2]
    n_iota = lax.broadcasted_iota(jnp.int32, (nbp, heads, 1), 0)
    for b in range(nbt):
        g = jnp.sum(km_ref[b] * q_ref[b][None], axis=-1, keepdims=True)
        for s in range(MOBA_TOPK):
            mx = jnp.max(g, axis=0)
            idx = jnp.min(jnp.where(g == mx[None], n_iota, nbp), axis=0)
            idx = jnp.minimum(idx, nbp - 1)
            idx_ref[b, s] = idx
            ok_ref[b, s] = (jnp.abs(mx) < INF).astype(jnp.int32)
            g = jnp.where(n_iota == idx[None], NEG_INF, g)


def _gate_topk(q3, kmean):
    b, heads, dh = q3.shape
    nbp = kmean.shape[1]
    out = jax.ShapeDtypeStruct((b, MOBA_TOPK, heads, 1), jnp.int32)
    out_spec = pl.BlockSpec((b, MOBA_TOPK, heads, 1), lambda i: (0, 0, 0, 0))
    return pl.pallas_call(
        _gate_topk_kernel,
        grid=(1,),
        in_specs=[pl.BlockSpec((b, heads, dh), lambda i: (0, 0, 0)),
                  pl.BlockSpec((b, nbp, heads, dh), lambda i: (0, 0, 0, 0))],
        out_specs=[out_spec, out_spec],
        out_shape=[out, out],
        compiler_params=_params("arbitrary"),
        name="gate_topk",
    )(q3, kmean)


GATHER_SLOTS = 4


def _moba_sample_kernel(pt_ref, idx_ref, ok_ref, q_ref, kn_ref, vn_ref, ck_ref, cv_ref, o_ref,
                        kbuf, vbuf, sem, *, heads, n_pages, ppb, scale):
    n_pairs = q_ref.shape[0]
    nsp, page, dh = kbuf.shape[1], kbuf.shape[2], kbuf.shape[3]
    rows = nsp * page

    def gather(pair, slot):
        b, h = pair // heads, pair % heads
        cps = []
        for s in range(MOBA_TOPK):
            blk = idx_ref[(b * MOBA_TOPK + s) * heads + h]
            for p in range(ppb):
                phys = pt_ref[b * n_pages + blk * ppb + p]
                j = s * ppb + p
                cps.append(pltpu.make_async_copy(ck_ref.at[phys, :, h, :], kbuf.at[slot, j], sem.at[0, slot]))
                cps.append(pltpu.make_async_copy(cv_ref.at[phys, :, h, :], vbuf.at[slot, j], sem.at[1, slot]))
        return cps

    n_slots = kbuf.shape[0]
    ahead = n_slots - 1
    for t in range(min(ahead, n_pairs)):
        for cp in gather(t, t):
            cp.start()
    row_blk = lax.broadcasted_iota(jnp.int32, (rows, 1), 0) // MOBA_BLOCK

    def body(pair, carry):
        slot = pair % n_slots

        @pl.when(pair + ahead < n_pairs)
        def _():
            for cp in gather(pair + ahead, (pair + ahead) % n_slots):
                cp.start()

        for cp in gather(pair, slot):
            cp.wait()
        b, h = pair // heads, pair % heads
        kk = kbuf[slot].reshape(rows, dh)
        vv = vbuf[slot].reshape(rows, dh)
        q = q_ref[pl.ds(pair, 1), :]
        s = jnp.sum(kk * q, axis=1, keepdims=True) * scale
        for j in range(MOBA_TOPK):
            ok = ok_ref[(b * MOBA_TOPK + j) * heads + h]
            s = s + jnp.where(row_blk == j, jnp.where(ok == 0, NEG_INF, 0.0), 0.0)
        s_own = jnp.sum(q * kn_ref[pl.ds(pair, 1), :], axis=1, keepdims=True) * scale
        mx = jnp.maximum(jnp.max(s, axis=0, keepdims=True), s_own)
        p = jnp.exp(s - mx)
        p_own = jnp.exp(s_own - mx)
        l = jnp.sum(p, axis=0, keepdims=True) + p_own
        o = jnp.sum(p * vv, axis=0, keepdims=True) + p_own * vn_ref[pl.ds(pair, 1), :]
        o_ref[pl.ds(pair, 1), :] = o / l
        return carry

    lax.fori_loop(0, n_pairs, body, 0)


def _moba_sample(page_table_flat, idx_flat, ok_flat, q2, kn2, vn2, cache_k, cache_v, *,
                 heads, n_pages, ppb):
    n_pairs, dh = q2.shape
    page = cache_k.shape[1]
    nsp = MOBA_TOPK * ppb
    full = pl.BlockSpec((n_pairs, dh), lambda i, pt, ix, ok: (0, 0))
    hbm = pl.BlockSpec(memory_space=pl.ANY)
    return pl.pallas_call(
        functools.partial(_moba_sample_kernel, heads=heads, n_pages=n_pages, ppb=ppb,
                          scale=dh ** -0.5),
        grid_spec=pltpu.PrefetchScalarGridSpec(
            num_scalar_prefetch=3,
            grid=(1,),
            in_specs=[full, full, full, hbm, hbm],
            out_specs=full,
            scratch_shapes=[
                pltpu.VMEM((GATHER_SLOTS, nsp, page, dh), F32),
                pltpu.VMEM((GATHER_SLOTS, nsp, page, dh), F32),
                pltpu.SemaphoreType.DMA((2, GATHER_SLOTS)),
            ],
        ),
        out_shape=jax.ShapeDtypeStruct((n_pairs, dh), F32),
        compiler_params=_params("arbitrary"),
        name="moba_sample",
    )(page_table_flat, idx_flat, ok_flat, q2, kn2, vn2, cache_k, cache_v)


def kernel(x_prompt, x_sample, cache_k, cache_v, state_pool, page_table, g_pool, w_pool, s_pool,
           g_mlp, w_up, w_down, g_kv, w_k, w_v, g_attn, w_q, w_o, g_final):
    b, t, d = x_prompt.shape
    bs, ts, _ = x_sample.shape
    n_phys, page, heads, dh = cache_k.shape
    hd = heads * dh
    depth = w_up.shape[0]
    n_pool = g_pool.shape[0]
    n_pages = page_table.shape[1]
    past_len = n_pages * page
    assert ts == 1 and MOBA_BLOCK % page == 0 and past_len % MOBA_BLOCK == 0
    assert t % MOBA_BLOCK == 0 and t >= POOL_STATE and dh == LANES
    assert past_len // MOBA_BLOCK >= MOBA_TOPK and heads <= LANES
    ppb = MOBA_BLOCK // page

    assert 1 <= n_pool < depth
    n_attn = depth - n_pool
    row = lambda a: a.reshape(1, -1)

    w_pool_b = w_pool.astype(BF16)
    wu_b = [w_up[0].astype(BF16)] + [None] * (depth - 1)
    wd_b = [w_down[0].astype(BF16)] + [None] * (depth - 1)
    side = [[(w_up, l + 1), (w_down, l + 1)] for l in range(depth - 1)]
    side[max(n_pool - 2, 0)] += [(w_k, None), (w_v, None)]
    for j in range(n_attn):
        side[n_pool + j - 1] += [(w_q, j), (w_o, j)]
    w_k_b = w_v_b = None
    w_q_b, w_o_b = [None] * n_attn, [None] * n_attn

    pt_flat = page_table.reshape(-1).astype(jnp.int32)
    kmean = _kmean(pt_flat, cache_k, batch=bs, n_pages=n_pages, ppb=ppb)

    xf = x_prompt.reshape(b * t, d)
    xs = x_sample.reshape(bs, d)
    pool_p, pool_s = [], []
    kp = vp = ks = vs = None
    for l in range(depth):
        if l < n_pool:
            xp, st = _pool_prompt(xf.reshape(b, t, d), row(g_pool[l]), w_pool_b, l, row(s_pool[l]),
                                  tp=512)
            pool_p.append(st)
            xf = xp.reshape(b * t, d)
            xs, st = _pool_sample(xs, state_pool, row(g_pool[l]), w_pool_b, l, row(s_pool[l]),
                                  pos0=past_len)
            pool_s.append(st)
        else:
            j = l - n_pool
            ga = row(g_attn[j])
            (q,) = _norm_proj(xf, ga, [w_q_b[j]], tm=512, tn=1024)
            a = _moba_prompt(q.reshape(b, t, hd), kp, vp, heads=heads)
            xf = _out_proj(a.reshape(b * t, hd), w_o_b[j], xf, tm=1024, tn=512)

            (q,) = _norm_proj(xs, ga, [w_q_b[j]], tm=bs, tn=1024)
            idx, ok = _gate_topk(q.reshape(bs, heads, dh), kmean)
            a = _moba_sample(pt_flat, idx.reshape(-1), ok.reshape(-1), q.reshape(bs * heads, dh),
                             ks.reshape(bs * heads, dh), vs.reshape(bs * heads, dh),
                             cache_k, cache_v, heads=heads, n_pages=n_pages, ppb=ppb)
            xs = _out_proj(a.reshape(bs, hd), w_o_b[j], xs, tm=bs, tn=1024)

        last = l == depth - 1
        jobs = [] if last else side[l]
        xf, xs, rounded = _mlp(xf, row(g_mlp[l]), row(g_final), wu_b[l], wd_b[l],
                               final_norm=last, tm=512, tf=512, extra=xs, casts=jobs)
        for (src, lead), w_b in zip(jobs, rounded):
            if src is w_up:
                wu_b[lead] = w_b
            elif src is w_down:
                wd_b[lead] = w_b
            elif src is w_k:
                w_k_b = w_b
            elif src is w_v:
                w_v_b = w_b
            elif src is w_q:
                w_q_b[lead] = w_b
            else:
                w_o_b[lead] = w_b

        if l == n_pool - 1:
            kf, vf = _norm_proj(xf, row(g_kv), [w_k_b, w_v_b], tm=512, tn=512)
            kp, vp = kf.reshape(b, t, hd), vf.reshape(b, t, hd)
            ks, vs = _norm_proj(xs, row(g_kv), [w_k_b, w_v_b], tm=bs, tn=1024)
    y_prompt = xf.reshape(b, t, d)
    y_sample = xs.reshape(bs, ts, d)

    return (y_prompt, y_sample,
            kp.reshape(b, t, heads, dh), vp.reshape(b, t, heads, dh), jnp.stack(pool_p, axis=0),
            ks.reshape(bs, ts, heads, dh), vs.reshape(bs, ts, heads, dh), jnp.stack(pool_s, axis=0))
```

```python
import functools

import jax
import jax.numpy as jnp
from jax import lax
from jax.experimental import pallas as pl
from jax.experimental.pallas import tpu as pltpu

EPS = 1e-6
POOL_WINDOWS = (2, 4, 8, 16)
POOL_STATE = max(POOL_WINDOWS) - 1
HALO = max(POOL_WINDOWS)
MOBA_BLOCK = 256
MOBA_TOPK = 3

BF16 = jnp.bfloat16
F32 = jnp.float32
NEG_INF = float("-inf")
INF = float("inf")
MASK_BIAS = 2.0 ** 100
LOG2E = 1.4426950408889634
LANES = 128
SUBLANES = 8
BF16_ROWS = 16
VMEM_LIMIT = 60 * 1024 * 1024

_NT = (((1,), (1,)), ((), ()))


def _params(*semantics):
    return pltpu.CompilerParams(dimension_semantics=semantics, vmem_limit_bytes=VMEM_LIMIT)


def _rms(x, g):
    ms = jnp.mean(x * x, axis=-1, keepdims=True)
    return x * lax.rsqrt(ms + EPS) * g


def _rms_rows(x_ref, g, store):
    n = x_ref.shape[0]
    step = min(n, BF16_ROWS)
    for r0 in range(0, n, step):
        parts = [_rms(x_ref[r:r + SUBLANES, :], g) for r in range(r0, r0 + step, SUBLANES)]
        store(slice(r0, r0 + step), parts[0] if len(parts) == 1 else jnp.concatenate(parts, axis=0))


def _pad_rows_bf16(x):
    pad = jnp.zeros((BF16_ROWS - x.shape[0], x.shape[1]), F32)
    return jnp.concatenate([x, pad], axis=0).astype(BF16)


def _mlp_kernel(x_ref, g_ref, gf_ref, wu_ref, wd_ref, *refs, final_norm, n_extra, n_cast):
    refs = list(refs)
    xe_ref = refs.pop(0) if n_extra else None
    cast_in = [refs.pop(0) for _ in range(n_cast)]
    o_ref = refs.pop(0)
    oe_ref = refs.pop(0) if n_extra else None
    cast_out = [refs.pop(0) for _ in range(n_cast)]
    (h_ref,) = refs
    i, j = pl.program_id(0), pl.program_id(1)
    tm = o_ref.shape[0]
    g = g_ref[...]

    @pl.when(j == 0)
    def _():
        pltpu.sync_copy(x_ref.at[pl.ds(pl.multiple_of(i * tm, tm), tm), :], o_ref)

        def store(rows, h):
            h_ref[rows, :] = h.astype(BF16)
        _rms_rows(o_ref, g, store)

    def step(rows):
        for src, dst in zip(cast_in, cast_out):
            dst[...] = src[...].astype(BF16)
        u = jnp.dot(h_ref[:rows, :], wu_ref[...], preferred_element_type=F32)
        u = jnp.maximum(u, 0.0)
        u = (u * u).astype(BF16)
        return jnp.dot(u, wd_ref[...], preferred_element_type=F32)

    if n_extra:
        @pl.when((i == 0) & (j == 0))
        def _():
            xe = xe_ref[...]
            h_ref[tm:, :] = _pad_rows_bf16(_rms(xe, g))
            oe_ref[...] = xe

        @pl.when(i == 0)
        def _():
            r = step(tm + BF16_ROWS)
            o_ref[...] += r[:tm]
            oe_ref[...] += r[tm:tm + n_extra]

        @pl.when(i > 0)
        def _():
            o_ref[...] += step(tm)
    else:
        o_ref[...] += step(tm)

    if final_norm:
        @pl.when(j == pl.num_programs(1) - 1)
        def _():
            def store(rows, y):
                o_ref[rows, :] = y
            _rms_rows(o_ref, gf_ref[...], store)
            if n_extra:
                @pl.when(i == 0)
                def _():
                    oe_ref[...] = _rms(oe_ref[...], gf_ref[...])


def _slab_plan(rows, cols, steps):
    best = None
    for cs in (1, 2, 4, 8):
        if cols % (cs * LANES):
            continue
        for rb in range(BF16_ROWS, rows + 1, BF16_ROWS):
            if rows % rb == 0 and (rows // rb) * cs <= steps:
                if best is None or rb * (cols // cs) < best[0] * (cols // best[1]):
                    best = (rb, cs)
                break
    assert best is not None, (rows, cols, steps)
    return best


def _slab_spec(rows, cols, steps, nj, lead):
    rb, cs = _slab_plan(rows, cols, steps)
    last = (rows // rb) * cs - 1

    def index(i, j):
        s = jnp.minimum(i * nj + j, last)
        return s // cs, s % cs

    if lead is None:
        spec_in = pl.BlockSpec((rb, cols // cs), index)
    else:
        spec_in = pl.BlockSpec((None, rb, cols // cs), lambda i, j: (lead,) + index(i, j))
    return spec_in, pl.BlockSpec((rb, cols // cs), index)


def _mlp(x, g, gf, w_up, w_down, *, final_norm, tm, tf, extra=None, casts=()):
    m, d = x.shape
    f = w_up.shape[1]
    tm = min(tm, m)
    tf = min(tf, f)
    ni, nj = m // tm, f // tf
    n_extra = 0 if extra is None else extra.shape[0]
    assert n_extra % SUBLANES == 0 and n_extra <= BF16_ROWS
    const = lambda i, j: (0, 0)
    in_specs = [
        pl.BlockSpec(memory_space=pl.ANY),
        pl.BlockSpec((1, d), const),
        pl.BlockSpec((1, d), const),
        pl.BlockSpec((d, tf), lambda i, j: (0, j)),
        pl.BlockSpec((tf, d), lambda i, j: (j, 0)),
    ]
    out_specs = [pl.BlockSpec((tm, d), lambda i, j: (i, 0), pipeline_mode=pl.Buffered(1))]
    out_shape = [jax.ShapeDtypeStruct((m, d), F32)]
    args = [x, g, gf, w_up, w_down]
    if n_extra:
        in_specs.append(pl.BlockSpec((n_extra, d), const))
        out_specs.append(pl.BlockSpec((n_extra, d), const))
        out_shape.append(jax.ShapeDtypeStruct((n_extra, d), F32))
        args.append(extra)
    for arr, lead in casts:
        rows, cols = arr.shape[-2:]
        spec_in, spec_out = _slab_spec(rows, cols, ni * nj, nj, lead)
        in_specs.append(spec_in)
        out_specs.append(spec_out)
        out_shape.append(jax.ShapeDtypeStruct((rows, cols), BF16))
        args.append(arr)
    outs = pl.pallas_call(
        functools.partial(_mlp_kernel, final_norm=final_norm, n_extra=n_extra, n_cast=len(casts)),
        grid=(ni, nj),
        in_specs=in_specs,
        out_specs=out_specs,
        out_shape=out_shape,
        scratch_shapes=[pltpu.VMEM((tm + (BF16_ROWS if n_extra else 0), d), BF16)],
        compiler_params=_params("arbitrary", "arbitrary"),
        name="mlp",
    )(*args)
    k = 2 if n_extra else 1
    return outs[0], (outs[1] if n_extra else None), list(outs[k:])


def _extra_col_index(nj):
    return lambda i, j: (0, jnp.where(i == 0, j, nj - 1))


def _norm_proj_kernel(x_ref, g_ref, *refs, n, n_extra):
    refs = list(refs)
    xe_ref = refs.pop(0) if n_extra else None
    w_refs = [refs.pop(0) for _ in range(n)]
    o_refs = [refs.pop(0) for _ in range(n)]
    oe_refs = [refs.pop(0) for _ in range(n)] if n_extra else []
    (h_ref,) = refs
    i, j = pl.program_id(0), pl.program_id(1)
    tm = x_ref.shape[0]
    g = g_ref[...]

    @pl.when(j == 0)
    def _():
        def store(rows, h):
            h_ref[rows, :] = h.astype(BF16)
        _rms_rows(x_ref, g, store)

    if n_extra:
        @pl.when((i == 0) & (j == 0))
        def _():
            h_ref[tm:, :] = _pad_rows_bf16(_rms(xe_ref[...], g))

    h = h_ref[:tm, :]
    for w_ref, o_ref in zip(w_refs, o_refs):
        o_ref[...] = jnp.dot(h, w_ref[...], preferred_element_type=F32)

    if n_extra:
        @pl.when(i == 0)
        def _():
            he = h_ref[tm:, :]
            for w_ref, oe_ref in zip(w_refs, oe_refs):
                oe_ref[...] = jnp.dot(he, w_ref[...], preferred_element_type=F32)[:n_extra]


def _norm_proj(x, g, weights, *, tm, tn, extra=None):
    m, d = x.shape
    nout = weights[0].shape[1]
    tm = min(tm, m)
    tn = min(tn, nout)
    n = len(weights)
    ni, nj = m // tm, nout // tn
    n_extra = 0 if extra is None else extra.shape[0]
    assert n_extra % SUBLANES == 0 and n_extra <= BF16_ROWS
    in_specs = [pl.BlockSpec((tm, d), lambda i, j: (i, 0)), pl.BlockSpec((1, d), lambda i, j: (0, 0))]
    out_specs = [pl.BlockSpec((tm, tn), lambda i, j: (i, j))] * n
    out_shape = [jax.ShapeDtypeStruct((m, nout), F32)] * n
    args = [x, g]
    if n_extra:
        in_specs.append(pl.BlockSpec((n_extra, d), lambda i, j: (0, 0)))
        out_specs += [pl.BlockSpec((n_extra, tn), _extra_col_index(nj))] * n
        out_shape += [jax.ShapeDtypeStruct((n_extra, nout), F32)] * n
        args.append(extra)
    in_specs += [pl.BlockSpec((d, tn), lambda i, j: (0, j))] * n
    return pl.pallas_call(
        functools.partial(_norm_proj_kernel, n=n, n_extra=n_extra),
        grid=(ni, nj),
        in_specs=in_specs,
        out_specs=out_specs,
        out_shape=out_shape,
        scratch_shapes=[pltpu.VMEM((tm + (BF16_ROWS if n_extra else 0), d), BF16)],
        compiler_params=_params("arbitrary", "arbitrary"),
        name="norm_proj",
    )(*args, *weights)


def _out_proj_kernel(a_ref, w_ref, x_ref, *refs, n_extra):
    if n_extra:
        ae_ref, xe_ref, o_ref, oe_ref = refs
    else:
        (o_ref,) = refs
    a = a_ref[...].astype(BF16)
    o_ref[...] = x_ref[...] + jnp.dot(a, w_ref[...], preferred_element_type=F32)

    if n_extra:
        @pl.when(pl.program_id(0) == 0)
        def _():
            r = jnp.dot(_pad_rows_bf16(ae_ref[...]), w_ref[...], preferred_element_type=F32)
            oe_ref[...] = xe_ref[...] + r[:n_extra]


def _out_proj(a, w_o, x, *, tm, tn, extra=None):
    m, d = x.shape
    k = a.shape[1]
    tm = min(tm, m)
    tn = min(tn, d)
    ni, nj = m // tm, d // tn
    n_extra = 0 if extra is None else extra[0].shape[0]
    assert n_extra % SUBLANES == 0 and n_extra <= BF16_ROWS
    in_specs = [
        pl.BlockSpec((tm, k), lambda i, j: (i, 0)),
        pl.BlockSpec((k, tn), lambda i, j: (0, j)),
        pl.BlockSpec((tm, tn), lambda i, j: (i, j)),
    ]
    out_specs = [pl.BlockSpec((tm, tn), lambda i, j: (i, j))]
    out_shape = [jax.ShapeDtypeStruct((m, d), F32)]
    args = [a, w_o, x]
    if n_extra:
        in_specs += [pl.BlockSpec((n_extra, k), lambda i, j: (0, 0)),
                     pl.BlockSpec((n_extra, tn), _extra_col_index(nj))]
        out_specs.append(pl.BlockSpec((n_extra, tn), _extra_col_index(nj)))
        out_shape.append(jax.ShapeDtypeStruct((n_extra, d), F32))
        args += list(extra)
    outs = pl.pallas_call(
        functools.partial(_out_proj_kernel, n_extra=n_extra),
        grid=(ni, nj),
        in_specs=in_specs,
        out_specs=out_specs,
        out_shape=out_shape,
        compiler_params=_params("arbitrary", "arbitrary"),
        name="out_proj",
    )(*args)
    return outs[0], (outs[1] if n_extra else None)


def _window_sums(ext, w, rows):
    s, span = ext, 1
    while span < w:
        s = s[span:] + s[:-span]
        span *= 2
    start = HALO + 1 - w
    return s[start:start + rows]


def _pool_prompt_kernel(x_ref, halo_ref, g_ref, w_ref, s_ref, o_ref, st_ref, *, tp):
    i = pl.program_id(1)
    x = x_ref[0]
    g = g_ref[...]
    h = _rms(x, g)
    hh = _rms(halo_ref[0], g)
    hh = jnp.where(i > 0, hh, 0.0)
    ext = jnp.concatenate([hh, h], axis=0)
    cg = x.shape[1] // len(POOL_WINDOWS)
    pos = i * tp + lax.broadcasted_iota(jnp.int32, (tp, 1), 0)
    for gi, w in enumerate(POOL_WINDOWS):
        c0, c1 = gi * cg, (gi + 1) * cg
        wsum = _window_sums(ext[:, c0:c1], w, tp)
        cnt = jnp.minimum(pos + 1, w).astype(F32)
        dlt = wsum * (1.0 / cnt) - h[:, c0:c1]
        y = jnp.dot(dlt.astype(BF16), w_ref[gi], preferred_element_type=F32)
        o_ref[0, :, c0:c1] = x[:, c0:c1] + y * s_ref[:, c0:c1]

    @pl.when(i == pl.num_programs(1) - 1)
    def _():
        st_ref[0] = h[tp - POOL_STATE:, :]


def _pool_prompt(x, g, w_pool, layer, s, *, tp):
    b, t, d = x.shape
    tp = min(tp, t)
    cg = d // len(POOL_WINDOWS)
    hb = tp // HALO
    return pl.pallas_call(
        functools.partial(_pool_prompt_kernel, tp=tp),
        grid=(b, t // tp),
        in_specs=[
            pl.BlockSpec((1, tp, d), lambda bi, i: (bi, i, 0)),
            pl.BlockSpec((1, HALO, d), lambda bi, i: (bi, jnp.maximum(i * hb - 1, 0), 0)),
            pl.BlockSpec((1, d), lambda bi, i: (0, 0)),
            pl.BlockSpec((None, len(POOL_WINDOWS), cg, cg), lambda bi, i: (layer, 0, 0, 0)),
            pl.BlockSpec((1, d), lambda bi, i: (0, 0)),
        ],
        out_specs=[
            pl.BlockSpec((1, tp, d), lambda bi, i: (bi, i, 0)),
            pl.BlockSpec((1, POOL_STATE, d), lambda bi, i: (bi, 0, 0)),
        ],
        out_shape=[jax.ShapeDtypeStruct((b, t, d), F32),
                   jax.ShapeDtypeStruct((b, POOL_STATE, d), F32)],
        compiler_params=_params("parallel", "arbitrary"),
        name="pool_prompt",
    )(x, x, g, w_pool, s)


def _pool_sample_kernel(x_ref, prev_ref, g_ref, w_ref, s_ref, o_ref, st_ref, d_ref, *, pos0):
    x = x_ref[...]
    h = _rms(x, g_ref[...])
    nb = x.shape[0]
    cg = x.shape[1] // len(POOL_WINDOWS)
    for b in range(nb):
        pv = prev_ref[b]
        hb = h[b:b + 1]
        st_ref[b, 0:POOL_STATE - 1, :] = pv[1:POOL_STATE]
        st_ref[b, POOL_STATE - 1:POOL_STATE, :] = hb
        for gi, w in enumerate(POOL_WINDOWS):
            c0, c1 = gi * cg, (gi + 1) * cg
            wsum = hb[:, c0:c1] + jnp.sum(pv[POOL_STATE + 1 - w:, c0:c1], axis=0, keepdims=True)
            cnt = float(min(pos0 + 1, w))
            d_ref[b:b + 1, c0:c1] = wsum / cnt - hb[:, c0:c1]
    for gi in range(len(POOL_WINDOWS)):
        c0, c1 = gi * cg, (gi + 1) * cg
        y = jnp.dot(d_ref[:, c0:c1].astype(BF16), w_ref[gi], preferred_element_type=F32)
        o_ref[:, c0:c1] = x[:, c0:c1] + y * s_ref[:, c0:c1]


def _pool_sample(x, prev, g, w_pool, layer, s, *, pos0):
    b, d = x.shape
    cg = d // len(POOL_WINDOWS)
    return pl.pallas_call(
        functools.partial(_pool_sample_kernel, pos0=pos0),
        grid=(1,),
        in_specs=[
            pl.BlockSpec((b, d), lambda i: (0, 0)),
            pl.BlockSpec((None, b, POOL_STATE, d), lambda i: (layer, 0, 0, 0)),
            pl.BlockSpec((1, d), lambda i: (0, 0)),
            pl.BlockSpec((None, len(POOL_WINDOWS), cg, cg), lambda i: (layer, 0, 0, 0)),
            pl.BlockSpec((1, d), lambda i: (0, 0)),
        ],
        out_specs=[
            pl.BlockSpec((b, d), lambda i: (0, 0)),
            pl.BlockSpec((b, POOL_STATE, d), lambda i: (0, 0, 0)),
        ],
        out_shape=[jax.ShapeDtypeStruct((b, d), F32),
                   jax.ShapeDtypeStruct((b, POOL_STATE, d), F32)],
        scratch_shapes=[pltpu.VMEM((b, d), F32)],
        compiler_params=_params("arbitrary"),
        name="pool_sample",
    )(x, prev, g, w_pool, s)


ROW_CHUNK = 64


def _moba_prompt_kernel(q_ref, k_ref, v_ref, o_ref, qa_ref, ka_ref, va_ref, s_ref, p_ref, *, scale):
    t, dh = q_ref.shape[1], q_ref.shape[2]
    blk = MOBA_BLOCK
    nb = t // blk
    nbp = -(-nb // SUBLANES) * SUBLANES
    cexp = scale * LOG2E

    k = k_ref[0]
    lane_t = lax.broadcasted_iota(jnp.int32, (t, LANES), 1)
    ka_ref[:dh, :] = k.T.astype(BF16)
    mask_row = lax.broadcasted_iota(jnp.int32, (LANES, t), 0)
    mask_key = lax.broadcasted_iota(jnp.int32, (LANES, t), 1) // blk
    ka_ref[dh:, :] = jnp.where(mask_row == mask_key, -MASK_BIAS, 0.0).astype(BF16)
    va_ref[:, :dh] = v_ref[0].astype(BF16)
    va_ref[:, dh:] = jnp.where(lane_t == 0, 1.0, 0.0).astype(BF16)
    means = [jnp.mean(k[n * blk:(n + 1) * blk], axis=0, keepdims=True) for n in range(nb)]
    if nbp > nb:
        means.append(jnp.zeros((nbp - nb, dh), F32))
    kmean = jnp.concatenate(means, axis=0)

    km_hi = kmean.astype(BF16)
    km_lo = kmean - km_hi.astype(F32)
    km_parts = jnp.concatenate([km_hi.astype(F32), km_lo], axis=0).astype(BF16)

    n_iota = lax.broadcasted_iota(jnp.int32, (nbp, blk), 0)
    lane_q = lax.broadcasted_iota(jnp.int32, (blk, LANES), 1)
    row_c = lax.broadcasted_iota(jnp.int32, (ROW_CHUNK, blk), 0)
    col_c = lax.broadcasted_iota(jnp.int32, (ROW_CHUNK, blk), 1)

    def scores(c):
        buf = c % 2
        qc = q_ref[0, c * blk:(c + 1) * blk, :]
        q_hi = qc.astype(BF16)
        qa_ref[buf, :, :dh] = q_hi
        if c > 0:
            q_lo = (qc - q_hi.astype(F32)).astype(BF16)
            g_hi = lax.dot_general(km_parts, q_hi, _NT, preferred_element_type=F32)
            g_lo = lax.dot_general(km_parts, q_lo, _NT, preferred_element_type=F32)
            gt = g_hi[:nbp] + g_hi[nbp:] + g_lo[:nbp]
            past = n_iota < c
            gt = jnp.where(past, gt, NEG_INF)
            rank = jnp.zeros((nbp, blk), jnp.int32)
            for m in range(c):
                rm = gt[m:m + 1, :]
                beats = (rm > gt) | ((rm == gt) & (n_iota > m))
                rank = rank + beats.astype(jnp.int32)
            sel_t = past & (rank < MOBA_TOPK) & (jnp.abs(gt) < INF)
            sel_t = jnp.concatenate(
                [sel_t.astype(F32), jnp.zeros((LANES - nbp, blk), F32)], axis=0)
            sel = sel_t.T
            qa_ref[buf, :, dh:] = jnp.where(lane_q < c, 1.0 - sel, 0.0).astype(BF16)
        else:
            qa_ref[buf, :, dh:] = jnp.zeros((blk, LANES), BF16)
        nk = (c + 1) * blk
        s_ref[buf, :, :nk] = jnp.dot(qa_ref[buf], ka_ref[:, :nk], preferred_element_type=F32)

    def weighted_values(c):
        nk = (c + 1) * blk
        o = jnp.dot(p_ref[c % 2, :, :nk], va_ref[:nk, :], preferred_element_type=F32)
        o_ref[0, c * blk:(c + 1) * blk, :] = (o[:, :dh] * (1.0 / o[:, dh:dh + 1])).astype(o_ref.dtype)

    scores(0)
    for c in range(nb):
        buf = c % 2
        nk = (c + 1) * blk
        if c + 1 < nb:
            scores(c + 1)
        for r in range(blk // ROW_CHUNK):
            r0 = r * ROW_CHUNK
            causal = col_c <= row_c + r0
            sd = jnp.where(causal, s_ref[buf, r0:r0 + ROW_CHUNK, c * blk:nk], -MASK_BIAS)
            mx = sd
            for n in range(c):
                mx = jnp.maximum(mx, s_ref[buf, r0:r0 + ROW_CHUNK, n * blk:(n + 1) * blk])
            mb = jnp.broadcast_to(jnp.max(mx, axis=1, keepdims=True), (ROW_CHUNK, blk))
            p_ref[buf, r0:r0 + ROW_CHUNK, c * blk:nk] = jnp.exp2((sd - mb) * cexp).astype(BF16)
            for n in range(c):
                sn = s_ref[buf, r0:r0 + ROW_CHUNK, n * blk:(n + 1) * blk]
                p_ref[buf, r0:r0 + ROW_CHUNK, n * blk:(n + 1) * blk] = (
                    jnp.exp2((sn - mb) * cexp).astype(BF16))
        weighted_values(c)


def _moba_prompt(q, k, v, *, heads):
    b, t, hd = q.shape
    dh = hd // heads
    spec = pl.BlockSpec((1, t, dh), lambda bi, h: (bi, 0, h))
    return pl.pallas_call(
        functools.partial(_moba_prompt_kernel, scale=dh ** -0.5),
        grid=(b, heads),
        in_specs=[spec, spec, spec],
        out_specs=spec,
        out_shape=jax.ShapeDtypeStruct((b, t, hd), BF16),
        scratch_shapes=[
            pltpu.VMEM((2, MOBA_BLOCK, dh + LANES), BF16),
            pltpu.VMEM((dh + LANES, t), BF16),
            pltpu.VMEM((t, dh + LANES), BF16),
            pltpu.VMEM((2, MOBA_BLOCK, t), F32),
            pltpu.VMEM((2, MOBA_BLOCK, t), BF16),
        ],
        compiler_params=_params("parallel", "parallel"),
        name="moba_prompt",
    )(q, k, v)


def _kmean_kernel(pt_ref, *refs, ppb):
    k_refs, o_ref = refs[:ppb], refs[ppb]
    n = pl.program_id(1)
    acc = jnp.sum(k_refs[0][0], axis=0)
    for r in k_refs[1:]:
        acc = acc + jnp.sum(r[0], axis=0)
    rows = ppb * k_refs[0].shape[1]
    o_ref[0, pl.ds(n, 1)] = (acc * (1.0 / rows))[None]


def _kmean(page_table_flat, cache_k, *, batch, n_pages, ppb):
    _, page, heads, dh = cache_k.shape
    nbp = n_pages // ppb
    specs = [
        pl.BlockSpec((1, page, heads, dh),
                     lambda b, n, pt, p=p: (pt[b * n_pages + n * ppb + p], 0, 0, 0))
        for p in range(ppb)
    ]
    return pl.pallas_call(
        functools.partial(_kmean_kernel, ppb=ppb),
        grid_spec=pltpu.PrefetchScalarGridSpec(
            num_scalar_prefetch=1,
            grid=(batch, nbp),
            in_specs=specs,
            out_specs=pl.BlockSpec((1, nbp, heads, dh), lambda b, n, pt: (b, 0, 0, 0)),
        ),
        out_shape=jax.ShapeDtypeStruct((batch, nbp, heads, dh), F32),
        compiler_params=_params("parallel", "arbitrary"),
        name="kmean",
    )(page_table_flat, *([cache_k] * ppb))


def _gate_topk_kernel(q_ref, km_ref, idx_ref, ok_ref):
    nbt = q_ref.shape[0]
    nbp, heads = km_ref.shape[1], km_ref.shape[2]
    n_iota = lax.broadcasted_iota(jnp.int32, (nbp, heads, 1), 0)
    for b in range(nbt):
        g = jnp.sum(km_ref[b] * q_ref[b][None], axis=-1, keepdims=True)
        for s in range(MOBA_TOPK):
            mx = jnp.max(g, axis=0)
            idx = jnp.min(jnp.where(g == mx[None], n_iota, nbp), axis=0)
            idx = jnp.minimum(idx, nbp - 1)
            idx_ref[b, s] = idx
            ok_ref[b, s] = (jnp.abs(mx) < INF).astype(jnp.int32)
            g = jnp.where(n_iota == idx[None], NEG_INF, g)


def _gate_topk(q3, kmean):
    b, heads, dh = q3.shape
    nbp = kmean.shape[1]
    out = jax.ShapeDtypeStruct((b, MOBA_TOPK, heads, 1), jnp.int32)
    out_spec = pl.BlockSpec((b, MOBA_TOPK, heads, 1), lambda i: (0, 0, 0, 0))
    return pl.pallas_call(
        _gate_topk_kernel,
        grid=(1,),
        in_specs=[pl.BlockSpec((b, heads, dh), lambda i: (0, 0, 0)),
                  pl.BlockSpec((b, nbp, heads, dh), lambda i: (0, 0, 0, 0))],
        out_specs=[out_spec, out_spec],
        out_shape=[out, out],
        compiler_params=_params("arbitrary"),
        name="gate_topk",
    )(q3, kmean)


GATHER_SLOTS = 4


def _moba_sample_kernel(pt_ref, idx_ref, ok_ref, q_ref, kn_ref, vn_ref, ck_ref, cv_ref, o_ref,
                        kbuf, vbuf, sem, *, heads, n_pages, ppb, scale):
    n_pairs = q_ref.shape[0]
    nsp, page, dh = kbuf.shape[1], kbuf.shape[2], kbuf.shape[3]
    rows = nsp * page

    def gather(pair, slot):
        b, h = pair // heads, pair % heads
        cps = []
        for s in range(MOBA_TOPK):
            blk = idx_ref[(b * MOBA_TOPK + s) * heads + h]
            for p in range(ppb):
                phys = pt_ref[b * n_pages + blk * ppb + p]
                j = s * ppb + p
                cps.append(pltpu.make_async_copy(ck_ref.at[phys, :, h, :], kbuf.at[slot, j], sem.at[0, slot]))
                cps.append(pltpu.make_async_copy(cv_ref.at[phys, :, h, :], vbuf.at[slot, j], sem.at[1, slot]))
        return cps

    n_slots = kbuf.shape[0]
    ahead = n_slots - 1
    for t in range(min(ahead, n_pairs)):
        for cp in gather(t, t):
            cp.start()
    row_blk = lax.broadcasted_iota(jnp.int32, (rows, 1), 0) // MOBA_BLOCK

    def body(pair, carry):
        slot = pair % n_slots

        @pl.when(pair + ahead < n_pairs)
        def _():
            for cp in gather(pair + ahead, (pair + ahead) % n_slots):
                cp.start()

        for cp in gather(pair, slot):
            cp.wait()
        b, h = pair // heads, pair % heads
        kk = kbuf[slot].reshape(rows, dh)
        vv = vbuf[slot].reshape(rows, dh)
        q = q_ref[pl.ds(pair, 1), :]
        s = jnp.sum(kk * q, axis=1, keepdims=True) * scale
        for j in range(MOBA_TOPK):
            ok = ok_ref[(b * MOBA_TOPK + j) * heads + h]
            s = s + jnp.where(row_blk == j, jnp.where(ok == 0, NEG_INF, 0.0), 0.0)
        s_own = jnp.sum(q * kn_ref[pl.ds(pair, 1), :], axis=1, keepdims=True) * scale
        mx = jnp.maximum(jnp.max(s, axis=0, keepdims=True), s_own)
        p = jnp.exp(s - mx)
        p_own = jnp.exp(s_own - mx)
        l = jnp.sum(p, axis=0, keepdims=True) + p_own
        o = jnp.sum(p * vv, axis=0, keepdims=True) + p_own * vn_ref[pl.ds(pair, 1), :]
        o_ref[pl.ds(pair, 1), :] = o / l
        return carry

    lax.fori_loop(0, n_pairs, body, 0)


def _moba_sample(page_table_flat, idx_flat, ok_flat, q2, kn2, vn2, cache_k, cache_v, *,
                 heads, n_pages, ppb):
    n_pairs, dh = q2.shape
    page = cache_k.shape[1]
    nsp = MOBA_TOPK * ppb
    full = pl.BlockSpec((n_pairs, dh), lambda i, pt, ix, ok: (0, 0))
    hbm = pl.BlockSpec(memory_space=pl.ANY)
    return pl.pallas_call(
        functools.partial(_moba_sample_kernel, heads=heads, n_pages=n_pages, ppb=ppb,
                          scale=dh ** -0.5),
        grid_spec=pltpu.PrefetchScalarGridSpec(
            num_scalar_prefetch=3,
            grid=(1,),
            in_specs=[full, full, full, hbm, hbm],
            out_specs=full,
            scratch_shapes=[
                pltpu.VMEM((GATHER_SLOTS, nsp, page, dh), F32),
                pltpu.VMEM((GATHER_SLOTS, nsp, page, dh), F32),
                pltpu.SemaphoreType.DMA((2, GATHER_SLOTS)),
            ],
        ),
        out_shape=jax.ShapeDtypeStruct((n_pairs, dh), F32),
        compiler_params=_params("arbitrary"),
        name="moba_sample",
    )(page_table_flat, idx_flat, ok_flat, q2, kn2, vn2, cache_k, cache_v)


def kernel(x_prompt, x_sample, cache_k, cache_v, state_pool, page_table, g_pool, w_pool, s_pool,
           g_mlp, w_up, w_down, g_kv, w_k, w_v, g_attn, w_q, w_o, g_final):
    b, t, d = x_prompt.shape
    bs, ts, _ = x_sample.shape
    n_phys, page, heads, dh = cache_k.shape
    hd = heads * dh
    depth = w_up.shape[0]
    n_pool = g_pool.shape[0]
    n_pages = page_table.shape[1]
    past_len = n_pages * page
    assert ts == 1 and MOBA_BLOCK % page == 0 and past_len % MOBA_BLOCK == 0
    assert t % MOBA_BLOCK == 0 and t >= POOL_STATE and dh == LANES
    assert past_len // MOBA_BLOCK >= MOBA_TOPK and heads <= LANES
    ppb = MOBA_BLOCK // page

    assert 1 <= n_pool < depth
    n_attn = depth - n_pool
    row = lambda a: a.reshape(1, -1)

    w_pool_b = w_pool.astype(BF16)
    wu_b = [w_up[0].astype(BF16)] + [None] * (depth - 1)
    wd_b = [w_down[0].astype(BF16)] + [None] * (depth - 1)
    side = [[(w_up, l + 1), (w_down, l + 1)] for l in range(depth - 1)]
    side[max(n_pool - 2, 0)] += [(w_k, None), (w_v, None)]
    for j in range(n_attn):
        side[n_pool + j - 1] += [(w_q, j), (w_o, j)]
    w_k_b = w_v_b = None
    w_q_b, w_o_b = [None] * n_attn, [None] * n_attn

    pt_flat = page_table.reshape(-1).astype(jnp.int32)
    kmean = _kmean(pt_flat, cache_k, batch=bs, n_pages=n_pages, ppb=ppb)

    xf = x_prompt.reshape(b * t, d)
    xs = x_sample.reshape(bs, d)
    pool_p, pool_s = [], []
    kp = vp = ks = vs = None
    for l in range(depth):
        if l < n_pool:
            xp, st = _pool_prompt(xf.reshape(b, t, d), row(g_pool[l]), w_pool_b, l, row(s_pool[l]),
                                  tp=512)
            pool_p.append(st)
            xf = xp.reshape(b * t, d)
            xs, st = _pool_sample(xs, state_pool, row(g_pool[l]), w_pool_b, l, row(s_pool[l]),
                                  pos0=past_len)
            pool_s.append(st)
        else:
            j = l - n_pool
            ga = row(g_attn[j])
            q, qs = _norm_proj(xf, ga, [w_q_b[j]], tm=512, tn=1024, extra=xs)
            a = _moba_prompt(q.reshape(b, t, hd), kp, vp, heads=heads)
            idx, ok = _gate_topk(qs.reshape(bs, heads, dh), kmean)
            a_s = _moba_sample(pt_flat, idx.reshape(-1), ok.reshape(-1), qs.reshape(bs * heads, dh),
                               ks.reshape(bs * heads, dh), vs.reshape(bs * heads, dh),
                               cache_k, cache_v, heads=heads, n_pages=n_pages, ppb=ppb)
            xf, xs = _out_proj(a.reshape(b * t, hd), w_o_b[j], xf, tm=1024, tn=512,
                               extra=(a_s.reshape(bs, hd), xs))

        last = l == depth - 1
        jobs = [] if last else side[l]
        xf, xs, rounded = _mlp(xf, row(g_mlp[l]), row(g_final), wu_b[l], wd_b[l],
                               final_norm=last, tm=512, tf=1024, extra=xs, casts=jobs)
        for (src, lead), w_b in zip(jobs, rounded):
            if src is w_up:
                wu_b[lead] = w_b
            elif src is w_down:
                wd_b[lead] = w_b
            elif src is w_k:
                w_k_b = w_b
            elif src is w_v:
                w_v_b = w_b
            elif src is w_q:
                w_q_b[lead] = w_b
            else:
                w_o_b[lead] = w_b

        if l == n_pool - 1:
            kf, vf, ks, vs = _norm_proj(xf, row(g_kv), [w_k_b, w_v_b], tm=512, tn=512, extra=xs)
            kp, vp = kf.reshape(b, t, hd), vf.reshape(b, t, hd)
    y_prompt = xf.reshape(b, t, d)
    y_sample = xs.reshape(bs, ts, d)

    return (y_prompt, y_sample,
            kp.reshape(b, t, heads, dh), vp.reshape(b, t, heads, dh), jnp.stack(pool_p, axis=0),
            ks.reshape(bs, ts, heads, dh), vs.reshape(bs, ts, heads, dh), jnp.stack(pool_s, axis=0))
```

```python
import functools

import jax
import jax.numpy as jnp
from jax import lax
from jax.experimental import pallas as pl
from jax.experimental.pallas import tpu as pltpu

EPS = 1e-6
POOL_WINDOWS = (2, 4, 8, 16)
POOL_STATE = max(POOL_WINDOWS) - 1
HALO = max(POOL_WINDOWS)
MOBA_BLOCK = 256
MOBA_TOPK = 3

BF16 = jnp.bfloat16
F32 = jnp.float32
NEG_INF = float("-inf")
INF = float("inf")
MASK_BIAS = 2.0 ** 100
LOG2E = 1.4426950408889634
LANES = 128
SUBLANES = 8
BF16_ROWS = 16
VMEM_LIMIT = 60 * 1024 * 1024

_NT = (((1,), (1,)), ((), ()))


def _params(*semantics):
    return pltpu.CompilerParams(dimension_semantics=semantics, vmem_limit_bytes=VMEM_LIMIT)


def _rms(x, g):
    ms = jnp.mean(x * x, axis=-1, keepdims=True)
    return x * lax.rsqrt(ms + EPS) * g


def _rms_rows(x_ref, g, store, start=0, stop=None):
    n = x_ref.shape[0] if stop is None else stop
    step = min(n - start, BF16_ROWS)
    for r0 in range(start, n, step):
        parts = [_rms(x_ref[r:r + SUBLANES, :], g) for r in range(r0, r0 + step, SUBLANES)]
        store(slice(r0, r0 + step), parts[0] if len(parts) == 1 else jnp.concatenate(parts, axis=0))


def _pad_rows_bf16(x):
    pad = jnp.zeros((BF16_ROWS - x.shape[0], x.shape[1]), F32)
    return jnp.concatenate([x, pad], axis=0).astype(BF16)


X_COPY_CHUNKS = 8


def _mlp_kernel(x_ref, g_ref, gf_ref, wu_ref, wd_ref, *refs, final_norm, n_extra, n_cast):
    refs = list(refs)
    xe_ref = refs.pop(0) if n_extra else None
    cast_in = [refs.pop(0) for _ in range(n_cast)]
    o_ref = refs.pop(0)
    oe_ref = refs.pop(0) if n_extra else None
    cast_out = [refs.pop(0) for _ in range(n_cast)]
    h_ref, x_sem = refs
    i, j = pl.program_id(0), pl.program_id(1)
    tm = o_ref.shape[0]
    g = g_ref[...]

    @pl.when(j == 0)
    def _():
        n_chunks = x_sem.shape[0]
        rows = tm // n_chunks
        base = pl.multiple_of(i * tm, tm)
        copies = [
            pltpu.make_async_copy(x_ref.at[pl.ds(base + c * rows, rows), :],
                                  o_ref.at[pl.ds(c * rows, rows), :], x_sem.at[c])
            for c in range(n_chunks)
        ]
        for cp in copies:
            cp.start()

        def store(rws, h):
            h_ref[rws, :] = h.astype(BF16)
        for c, cp in enumerate(copies):
            cp.wait()
            _rms_rows(o_ref, g, store, c * rows, (c + 1) * rows)

    def step(rows):
        for src, dst in zip(cast_in, cast_out):
            dst[...] = src[...].astype(BF16)
        u = jnp.dot(h_ref[:rows, :], wu_ref[...], preferred_element_type=F32)
        u = jnp.maximum(u, 0.0)
        u = (u * u).astype(BF16)
        return jnp.dot(u, wd_ref[...], preferred_element_type=F32)

    if n_extra:
        @pl.when((i == 0) & (j == 0))
        def _():
            xe = xe_ref[...]
            h_ref[tm:, :] = _pad_rows_bf16(_rms(xe, g))
            oe_ref[...] = xe

        @pl.when(i == 0)
        def _():
            r = step(tm + BF16_ROWS)
            o_ref[...] += r[:tm]
            oe_ref[...] += r[tm:tm + n_extra]

        @pl.when(i > 0)
        def _():
            o_ref[...] += step(tm)
    else:
        o_ref[...] += step(tm)

    if final_norm:
        @pl.when(j == pl.num_programs(1) - 1)
        def _():
            def store(rows, y):
                o_ref[rows, :] = y
            _rms_rows(o_ref, gf_ref[...], store)
            if n_extra:
                @pl.when(i == 0)
                def _():
                    oe_ref[...] = _rms(oe_ref[...], gf_ref[...])


def _slab_plan(rows, cols, steps):
    best = None
    for cs in (1, 2, 4, 8):
        if cols % (cs * LANES):
            continue
        for rb in range(BF16_ROWS, rows + 1, BF16_ROWS):
            if rows % rb == 0 and (rows // rb) * cs <= steps:
                if best is None or rb * (cols // cs) < best[0] * (cols // best[1]):
                    best = (rb, cs)
                break
    assert best is not None, (rows, cols, steps)
    return best


def _slab_spec(rows, cols, steps, nj, lead):
    rb, cs = _slab_plan(rows, cols, steps)
    last = (rows // rb) * cs - 1

    def index(i, j):
        s = jnp.minimum(i * nj + j, last)
        return s // cs, s % cs

    if lead is None:
        spec_in = pl.BlockSpec((rb, cols // cs), index)
    else:
        spec_in = pl.BlockSpec((None, rb, cols // cs), lambda i, j: (lead,) + index(i, j))
    return spec_in, pl.BlockSpec((rb, cols // cs), index)


def _mlp(x, g, gf, w_up, w_down, *, final_norm, tm, tf, extra=None, casts=()):
    m, d = x.shape
    f = w_up.shape[1]
    tm = min(tm, m)
    tf = min(tf, f)
    ni, nj = m // tm, f // tf
    n_extra = 0 if extra is None else extra.shape[0]
    assert n_extra % SUBLANES == 0 and n_extra <= BF16_ROWS
    const = lambda i, j: (0, 0)
    in_specs = [
        pl.BlockSpec(memory_space=pl.ANY),
        pl.BlockSpec((1, d), const),
        pl.BlockSpec((1, d), const),
        pl.BlockSpec((d, tf), lambda i, j: (0, j)),
        pl.BlockSpec((tf, d), lambda i, j: (j, 0)),
    ]
    out_specs = [pl.BlockSpec((tm, d), lambda i, j: (i, 0), pipeline_mode=pl.Buffered(1))]
    out_shape = [jax.ShapeDtypeStruct((m, d), F32)]
    args = [x, g, gf, w_up, w_down]
    if n_extra:
        in_specs.append(pl.BlockSpec((n_extra, d), const))
        out_specs.append(pl.BlockSpec((n_extra, d), const))
        out_shape.append(jax.ShapeDtypeStruct((n_extra, d), F32))
        args.append(extra)
    for arr, lead in casts:
        rows, cols = arr.shape[-2:]
        spec_in, spec_out = _slab_spec(rows, cols, ni * nj, nj, lead)
        in_specs.append(spec_in)
        out_specs.append(spec_out)
        out_shape.append(jax.ShapeDtypeStruct((rows, cols), BF16))
        args.append(arr)
    outs = pl.pallas_call(
        functools.partial(_mlp_kernel, final_norm=final_norm, n_extra=n_extra, n_cast=len(casts)),
        grid=(ni, nj),
        in_specs=in_specs,
        out_specs=out_specs,
        out_shape=out_shape,
        scratch_shapes=[pltpu.VMEM((tm + (BF16_ROWS if n_extra else 0), d), BF16),
                        pltpu.SemaphoreType.DMA((X_COPY_CHUNKS if tm % (X_COPY_CHUNKS * BF16_ROWS) == 0 else 1,))],
        compiler_params=_params("arbitrary", "arbitrary"),
        name="mlp",
    )(*args)
    k = 2 if n_extra else 1
    return outs[0], (outs[1] if n_extra else None), list(outs[k:])


def _extra_col_index(nj):
    return lambda i, j: (0, jnp.where(i == 0, j, nj - 1))


def _norm_proj_kernel(x_ref, g_ref, *refs, n, n_extra):
    refs = list(refs)
    xe_ref = refs.pop(0) if n_extra else None
    w_refs = [refs.pop(0) for _ in range(n)]
    o_refs = [refs.pop(0) for _ in range(n)]
    oe_refs = [refs.pop(0) for _ in range(n)] if n_extra else []
    (h_ref,) = refs
    i, j = pl.program_id(0), pl.program_id(1)
    tm = x_ref.shape[0]
    g = g_ref[...]

    @pl.when(j == 0)
    def _():
        def store(rows, h):
            h_ref[rows, :] = h.astype(BF16)
        _rms_rows(x_ref, g, store)

    if n_extra:
        @pl.when((i == 0) & (j == 0))
        def _():
            h_ref[tm:, :] = _pad_rows_bf16(_rms(xe_ref[...], g))

    h = h_ref[:tm, :]
    for w_ref, o_ref in zip(w_refs, o_refs):
        o_ref[...] = jnp.dot(h, w_ref[...], preferred_element_type=F32)

    if n_extra:
        @pl.when(i == 0)
        def _():
            he = h_ref[tm:, :]
            for w_ref, oe_ref in zip(w_refs, oe_refs):
                oe_ref[...] = jnp.dot(he, w_ref[...], preferred_element_type=F32)[:n_extra]


def _norm_proj(x, g, weights, *, tm, tn, extra=None):
    m, d = x.shape
    nout = weights[0].shape[1]
    tm = min(tm, m)
    tn = min(tn, nout)
    n = len(weights)
    ni, nj = m // tm, nout // tn
    n_extra = 0 if extra is None else extra.shape[0]
    assert n_extra % SUBLANES == 0 and n_extra <= BF16_ROWS
    in_specs = [pl.BlockSpec((tm, d), lambda i, j: (i, 0)), pl.BlockSpec((1, d), lambda i, j: (0, 0))]
    out_specs = [pl.BlockSpec((tm, tn), lambda i, j: (i, j))] * n
    out_shape = [jax.ShapeDtypeStruct((m, nout), F32)] * n
    args = [x, g]
    if n_extra:
        in_specs.append(pl.BlockSpec((n_extra, d), lambda i, j: (0, 0)))
        out_specs += [pl.BlockSpec((n_extra, tn), _extra_col_index(nj))] * n
        out_shape += [jax.ShapeDtypeStruct((n_extra, nout), F32)] * n
        args.append(extra)
    in_specs += [pl.BlockSpec((d, tn), lambda i, j: (0, j))] * n
    return pl.pallas_call(
        functools.partial(_norm_proj_kernel, n=n, n_extra=n_extra),
        grid=(ni, nj),
        in_specs=in_specs,
        out_specs=out_specs,
        out_shape=out_shape,
        scratch_shapes=[pltpu.VMEM((tm + (BF16_ROWS if n_extra else 0), d), BF16)],
        compiler_params=_params("arbitrary", "arbitrary"),
        name="norm_proj",
    )(*args, *weights)


def _out_proj_kernel(a_ref, w_ref, x_ref, *refs, n_extra):
    if n_extra:
        ae_ref, xe_ref, o_ref, oe_ref = refs
    else:
        (o_ref,) = refs
    a = a_ref[...].astype(BF16)
    o_ref[...] = x_ref[...] + jnp.dot(a, w_ref[...], preferred_element_type=F32)

    if n_extra:
        @pl.when(pl.program_id(0) == 0)
        def _():
            r = jnp.dot(_pad_rows_bf16(ae_ref[...]), w_ref[...], preferred_element_type=F32)
            oe_ref[...] = xe_ref[...] + r[:n_extra]


def _out_proj(a, w_o, x, *, tm, tn, extra=None):
    m, d = x.shape
    k = a.shape[1]
    tm = min(tm, m)
    tn = min(tn, d)
    ni, nj = m // tm, d // tn
    n_extra = 0 if extra is None else extra[0].shape[0]
    assert n_extra % SUBLANES == 0 and n_extra <= BF16_ROWS
    in_specs = [
        pl.BlockSpec((tm, k), lambda i, j: (i, 0)),
        pl.BlockSpec((k, tn), lambda i, j: (0, j)),
        pl.BlockSpec((tm, tn), lambda i, j: (i, j)),
    ]
    out_specs = [pl.BlockSpec((tm, tn), lambda i, j: (i, j))]
    out_shape = [jax.ShapeDtypeStruct((m, d), F32)]
    args = [a, w_o, x]
    if n_extra:
        in_specs += [pl.BlockSpec((n_extra, k), lambda i, j: (0, 0)),
                     pl.BlockSpec((n_extra, tn), _extra_col_index(nj))]
        out_specs.append(pl.BlockSpec((n_extra, tn), _extra_col_index(nj)))
        out_shape.append(jax.ShapeDtypeStruct((n_extra, d), F32))
        args += list(extra)
    outs = pl.pallas_call(
        functools.partial(_out_proj_kernel, n_extra=n_extra),
        grid=(ni, nj),
        in_specs=in_specs,
        out_specs=out_specs,
        out_shape=out_shape,
        compiler_params=_params("arbitrary", "arbitrary"),
        name="out_proj",
    )(*args)
    return outs[0], (outs[1] if n_extra else None)


def _window_sums(ext, w, rows):
    s, span = ext, 1
    while span < w:
        s = s[span:] + s[:-span]
        span *= 2
    start = HALO + 1 - w
    return s[start:start + rows]


def _pool_prompt_kernel(x_ref, halo_ref, g_ref, w_ref, s_ref, o_ref, st_ref, *, tp):
    i = pl.program_id(1)
    x = x_ref[0]
    g = g_ref[...]
    h = _rms(x, g)
    hh = _rms(halo_ref[0], g)
    hh = jnp.where(i > 0, hh, 0.0)
    ext = jnp.concatenate([hh, h], axis=0)
    cg = x.shape[1] // len(POOL_WINDOWS)
    pos = i * tp + lax.broadcasted_iota(jnp.int32, (tp, 1), 0)
    for gi, w in enumerate(POOL_WINDOWS):
        c0, c1 = gi * cg, (gi + 1) * cg
        wsum = _window_sums(ext[:, c0:c1], w, tp)
        cnt = jnp.minimum(pos + 1, w).astype(F32)
        dlt = wsum * (1.0 / cnt) - h[:, c0:c1]
        y = jnp.dot(dlt.astype(BF16), w_ref[gi], preferred_element_type=F32)
        o_ref[0, :, c0:c1] = x[:, c0:c1] + y * s_ref[:, c0:c1]

    @pl.when(i == pl.num_programs(1) - 1)
    def _():
        st_ref[0] = h[tp - POOL_STATE:, :]


def _pool_prompt(x, g, w_pool, layer, s, *, tp):
    b, t, d = x.shape
    tp = min(tp, t)
    cg = d // len(POOL_WINDOWS)
    hb = tp // HALO
    return pl.pallas_call(
        functools.partial(_pool_prompt_kernel, tp=tp),
        grid=(b, t // tp),
        in_specs=[
            pl.BlockSpec((1, tp, d), lambda bi, i: (bi, i, 0)),
            pl.BlockSpec((1, HALO, d), lambda bi, i: (bi, jnp.maximum(i * hb - 1, 0), 0)),
            pl.BlockSpec((1, d), lambda bi, i: (0, 0)),
            pl.BlockSpec((None, len(POOL_WINDOWS), cg, cg), lambda bi, i: (layer, 0, 0, 0)),
            pl.BlockSpec((1, d), lambda bi, i: (0, 0)),
        ],
        out_specs=[
            pl.BlockSpec((1, tp, d), lambda bi, i: (bi, i, 0)),
            pl.BlockSpec((1, POOL_STATE, d), lambda bi, i: (bi, 0, 0)),
        ],
        out_shape=[jax.ShapeDtypeStruct((b, t, d), F32),
                   jax.ShapeDtypeStruct((b, POOL_STATE, d), F32)],
        compiler_params=_params("parallel", "arbitrary"),
        name="pool_prompt",
    )(x, x, g, w_pool, s)


def _pool_sample_kernel(x_ref, prev_ref, g_ref, w_ref, s_ref, o_ref, st_ref, d_ref, *, pos0):
    x = x_ref[...]
    h = _rms(x, g_ref[...])
    nb = x.shape[0]
    cg = x.shape[1] // len(POOL_WINDOWS)
    for b in range(nb):
        pv = prev_ref[b]
        hb = h[b:b + 1]
        st_ref[b, 0:POOL_STATE - 1, :] = pv[1:POOL_STATE]
        st_ref[b, POOL_STATE - 1:POOL_STATE, :] = hb
        for gi, w in enumerate(POOL_WINDOWS):
            c0, c1 = gi * cg, (gi + 1) * cg
            wsum = hb[:, c0:c1] + jnp.sum(pv[POOL_STATE + 1 - w:, c0:c1], axis=0, keepdims=True)
            cnt = float(min(pos0 + 1, w))
            d_ref[b:b + 1, c0:c1] = wsum / cnt - hb[:, c0:c1]
    for gi in range(len(POOL_WINDOWS)):
        c0, c1 = gi * cg, (gi + 1) * cg
        y = jnp.dot(d_ref[:, c0:c1].astype(BF16), w_ref[gi], preferred_element_type=F32)
        o_ref[:, c0:c1] = x[:, c0:c1] + y * s_ref[:, c0:c1]


def _pool_sample(x, prev, g, w_pool, layer, s, *, pos0):
    b, d = x.shape
    cg = d // len(POOL_WINDOWS)
    return pl.pallas_call(
        functools.partial(_pool_sample_kernel, pos0=pos0),
        grid=(1,),
        in_specs=[
            pl.BlockSpec((b, d), lambda i: (0, 0)),
            pl.BlockSpec((None, b, POOL_STATE, d), lambda i: (layer, 0, 0, 0)),
            pl.BlockSpec((1, d), lambda i: (0, 0)),
            pl.BlockSpec((None, len(POOL_WINDOWS), cg, cg), lambda i: (layer, 0, 0, 0)),
            pl.BlockSpec((1, d), lambda i: (0, 0)),
        ],
        out_specs=[
            pl.BlockSpec((b, d), lambda i: (0, 0)),
            pl.BlockSpec((b, POOL_STATE, d), lambda i: (0, 0, 0)),
        ],
        out_shape=[jax.ShapeDtypeStruct((b, d), F32),
                   jax.ShapeDtypeStruct((b, POOL_STATE, d), F32)],
        scratch_shapes=[pltpu.VMEM((b, d), F32)],
        compiler_params=_params("arbitrary"),
        name="pool_sample",
    )(x, prev, g, w_pool, s)


ROW_CHUNK = 64


def _moba_prompt_kernel(q_ref, k_ref, v_ref, o_ref, qa_ref, ka_ref, va_ref, s_ref, p_ref, *, scale):
    t, dh = q_ref.shape[1], q_ref.shape[2]
    blk = MOBA_BLOCK
    nb = t // blk
    nbp = -(-nb // SUBLANES) * SUBLANES
    cexp = scale * LOG2E

    k = k_ref[0]
    lane_t = lax.broadcasted_iota(jnp.int32, (t, LANES), 1)
    ka_ref[:dh, :] = k.T.astype(BF16)
    mask_row = lax.broadcasted_iota(jnp.int32, (LANES, t), 0)
    mask_key = lax.broadcasted_iota(jnp.int32, (LANES, t), 1) // blk
    ka_ref[dh:, :] = jnp.where(mask_row == mask_key, -MASK_BIAS, 0.0).astype(BF16)
    va_ref[:, :dh] = v_ref[0].astype(BF16)
    va_ref[:, dh:] = jnp.where(lane_t == 0, 1.0, 0.0).astype(BF16)
    means = [jnp.mean(k[n * blk:(n + 1) * blk], axis=0, keepdims=True) for n in range(nb)]
    if nbp > nb:
        means.append(jnp.zeros((nbp - nb, dh), F32))
    kmean = jnp.concatenate(means, axis=0)

    km_hi = kmean.astype(BF16)
    km_lo = kmean - km_hi.astype(F32)
    km_parts = jnp.concatenate([km_hi.astype(F32), km_lo], axis=0).astype(BF16)

    n_iota = lax.broadcasted_iota(jnp.int32, (nbp, blk), 0)
    lane_q = lax.broadcasted_iota(jnp.int32, (blk, LANES), 1)
    row_c = lax.broadcasted_iota(jnp.int32, (ROW_CHUNK, blk), 0)
    col_c = lax.broadcasted_iota(jnp.int32, (ROW_CHUNK, blk), 1)

    def augment(c):
        buf = c
        qc = q_ref[0, c * blk:(c + 1) * blk, :]
        q_hi = qc.astype(BF16)
        qa_ref[buf, :, :dh] = q_hi
        if c > 0:
            q_lo = (qc - q_hi.astype(F32)).astype(BF16)
            g_hi = lax.dot_general(km_parts, q_hi, _NT, preferred_element_type=F32)
            g_lo = lax.dot_general(km_parts, q_lo, _NT, preferred_element_type=F32)
            gt = g_hi[:nbp] + g_hi[nbp:] + g_lo[:nbp]
            past = n_iota < c
            gt = jnp.where(past, gt, NEG_INF)
            rank = jnp.zeros((nbp, blk), jnp.int32)
            for m in range(c):
                rm = gt[m:m + 1, :]
                beats = (rm > gt) | ((rm == gt) & (n_iota > m))
                rank = rank + beats.astype(jnp.int32)
            sel_t = past & (rank < MOBA_TOPK) & (jnp.abs(gt) < INF)
            sel_t = jnp.concatenate(
                [sel_t.astype(F32), jnp.zeros((LANES - nbp, blk), F32)], axis=0)
            sel = sel_t.T
            qa_ref[buf, :, dh:] = jnp.where(lane_q < c, 1.0 - sel, 0.0).astype(BF16)
        else:
            qa_ref[buf, :, dh:] = jnp.zeros((blk, LANES), BF16)

    def scores(c):
        nk = (c + 1) * blk
        s_ref[c % 2, :, :nk] = jnp.dot(qa_ref[c], ka_ref[:, :nk], preferred_element_type=F32)

    def weighted_values(c):
        nk = (c + 1) * blk
        o = jnp.dot(p_ref[c % 2, :, :nk], va_ref[:nk, :], preferred_element_type=F32)
        o_ref[0, c * blk:(c + 1) * blk, :] = (o[:, :dh] * (1.0 / o[:, dh:dh + 1])).astype(o_ref.dtype)

    for c in range(nb):
        augment(c)
    scores(0)
    for c in range(nb):
        buf = c % 2
        nk = (c + 1) * blk
        if c + 1 < nb:
            scores(c + 1)
        if c > 0:
            weighted_values(c - 1)
        for r in range(blk // ROW_CHUNK):
            r0 = r * ROW_CHUNK
            causal = col_c <= row_c + r0
            sd = jnp.where(causal, s_ref[buf, r0:r0 + ROW_CHUNK, c * blk:nk], -MASK_BIAS)
            mx = sd
            for n in range(c):
                mx = jnp.maximum(mx, s_ref[buf, r0:r0 + ROW_CHUNK, n * blk:(n + 1) * blk])
            mb = jnp.broadcast_to(jnp.max(mx, axis=1, keepdims=True), (ROW_CHUNK, blk))
            p_ref[buf, r0:r0 + ROW_CHUNK, c * blk:nk] = jnp.exp2((sd - mb) * cexp).astype(BF16)
            for n in range(c):
                sn = s_ref[buf, r0:r0 + ROW_CHUNK, n * blk:(n + 1) * blk]
                p_ref[buf, r0:r0 + ROW_CHUNK, n * blk:(n + 1) * blk] = (
                    jnp.exp2((sn - mb) * cexp).astype(BF16))
    weighted_values(nb - 1)


def _moba_prompt(q, k, v, *, heads):
    b, t, hd = q.shape
    dh = hd // heads
    spec = pl.BlockSpec((1, t, dh), lambda bi, h: (bi, 0, h))
    return pl.pallas_call(
        functools.partial(_moba_prompt_kernel, scale=dh ** -0.5),
        grid=(b, heads),
        in_specs=[spec, spec, spec],
        out_specs=spec,
        out_shape=jax.ShapeDtypeStruct((b, t, hd), BF16),
        scratch_shapes=[
            pltpu.VMEM((t // MOBA_BLOCK, MOBA_BLOCK, dh + LANES), BF16),
            pltpu.VMEM((dh + LANES, t), BF16),
            pltpu.VMEM((t, dh + LANES), BF16),
            pltpu.VMEM((2, MOBA_BLOCK, t), F32),
            pltpu.VMEM((2, MOBA_BLOCK, t), BF16),
        ],
        compiler_params=_params("parallel", "parallel"),
        name="moba_prompt",
    )(q, k, v)


def _kmean_kernel(pt_ref, *refs, ppb):
    k_refs, o_ref = refs[:ppb], refs[ppb]
    n = pl.program_id(1)
    acc = jnp.sum(k_refs[0][0], axis=0)
    for r in k_refs[1:]:
        acc = acc + jnp.sum(r[0], axis=0)
    rows = ppb * k_refs[0].shape[1]
    o_ref[0, pl.ds(n, 1)] = (acc * (1.0 / rows))[None]


def _kmean(page_table_flat, cache_k, *, batch, n_pages, ppb):
    _, page, heads, dh = cache_k.shape
    nbp = n_pages // ppb
    specs = [
        pl.BlockSpec((1, page, heads, dh),
                     lambda b, n, pt, p=p: (pt[b * n_pages + n * ppb + p], 0, 0, 0))
        for p in range(ppb)
    ]
    return pl.pallas_call(
        functools.partial(_kmean_kernel, ppb=ppb),
        grid_spec=pltpu.PrefetchScalarGridSpec(
            num_scalar_prefetch=1,
            grid=(batch, nbp),
            in_specs=specs,
            out_specs=pl.BlockSpec((1, nbp, heads, dh), lambda b, n, pt: (b, 0, 0, 0)),
        ),
        out_shape=jax.ShapeDtypeStruct((batch, nbp, heads, dh), F32),
        compiler_params=_params("parallel", "arbitrary"),
        name="kmean",
    )(page_table_flat, *([cache_k] * ppb))


def _gate_topk_kernel(q_ref, km_ref, idx_ref, ok_ref):
    nbt = q_ref.shape[0]
    nbp, heads = km_ref.shape[1], km_ref.shape[2]
    n_iota = lax.broadcasted_iota(jnp.int32, (nbp, heads, 1), 0)
    for b in range(nbt):
        g = jnp.sum(km_ref[b] * q_ref[b][None], axis=-1, keepdims=True)
        for s in range(MOBA_TOPK):
            mx = jnp.max(g, axis=0)
            idx = jnp.min(jnp.where(g == mx[None], n_iota, nbp), axis=0)
            idx = jnp.minimum(idx, nbp - 1)
            idx_ref[b, s] = idx
            ok_ref[b, s] = (jnp.abs(mx) < INF).astype(jnp.int32)
            g = jnp.where(n_iota == idx[None], NEG_INF, g)


def _gate_topk(q3, kmean):
    b, heads, dh = q3.shape
    nbp = kmean.shape[1]
    out = jax.ShapeDtypeStruct((b, MOBA_TOPK, heads, 1), jnp.int32)
    out_spec = pl.BlockSpec((b, MOBA_TOPK, heads, 1), lambda i: (0, 0, 0, 0))
    return pl.pallas_call(
        _gate_topk_kernel,
        grid=(1,),
        in_specs=[pl.BlockSpec((b, heads, dh), lambda i: (0, 0, 0)),
                  pl.BlockSpec((b, nbp, heads, dh), lambda i: (0, 0, 0, 0))],
        out_specs=[out_spec, out_spec],
        out_shape=[out, out],
        compiler_params=_params("arbitrary"),
        name="gate_topk",
    )(q3, kmean)


GATHER_SLOTS = 8


def _moba_sample_kernel(pt_ref, idx_ref, ok_ref, q_ref, kn_ref, vn_ref, ck_ref, cv_ref, o_ref,
                        kbuf, vbuf, sem, *, heads, n_pages, ppb, scale):
    n_pairs = q_ref.shape[0]
    nsp, page, dh = kbuf.shape[1], kbuf.shape[2], kbuf.shape[3]
    rows = nsp * page

    def gather(pair, slot):
        b, h = pair // heads, pair % heads
        cps = []
        for s in range(MOBA_TOPK):
            blk = idx_ref[(b * MOBA_TOPK + s) * heads + h]
            for p in range(ppb):
                phys = pt_ref[b * n_pages + blk * ppb + p]
                j = s * ppb + p
                cps.append(pltpu.make_async_copy(ck_ref.at[phys, :, h, :], kbuf.at[slot, j], sem.at[0, slot]))
                cps.append(pltpu.make_async_copy(cv_ref.at[phys, :, h, :], vbuf.at[slot, j], sem.at[1, slot]))
        return cps

    n_slots = kbuf.shape[0]
    ahead = n_slots - 1
    for t in range(min(ahead, n_pairs)):
        for cp in gather(t, t):
            cp.start()
    row_blk = lax.broadcasted_iota(jnp.int32, (rows, 1), 0) // MOBA_BLOCK

    def body(pair, carry):
        slot = pair % n_slots

        @pl.when(pair + ahead < n_pairs)
        def _():
            for cp in gather(pair + ahead, (pair + ahead) % n_slots):
                cp.start()

        for cp in gather(pair, slot):
            cp.wait()
        b, h = pair // heads, pair % heads
        kk = kbuf[slot].reshape(rows, dh)
        vv = vbuf[slot].reshape(rows, dh)
        q = q_ref[pl.ds(pair, 1), :]
        s = jnp.sum(kk * q, axis=1, keepdims=True) * scale
        for j in range(MOBA_TOPK):
            ok = ok_ref[(b * MOBA_TOPK + j) * heads + h]
            s = s + jnp.where(row_blk == j, jnp.where(ok == 0, NEG_INF, 0.0), 0.0)
        s_own = jnp.sum(q * kn_ref[pl.ds(pair, 1), :], axis=1, keepdims=True) * scale
        mx = jnp.maximum(jnp.max(s, axis=0, keepdims=True), s_own)
        p = jnp.exp(s - mx)
        p_own = jnp.exp(s_own - mx)
        l = jnp.sum(p, axis=0, keepdims=True) + p_own
        o = jnp.sum(p * vv, axis=0, keepdims=True) + p_own * vn_ref[pl.ds(pair, 1), :]
        o_ref[pl.ds(pair, 1), :] = o / l
        return carry

    lax.fori_loop(0, n_pairs, body, 0)


def _moba_sample(page_table_flat, idx_flat, ok_flat, q2, kn2, vn2, cache_k, cache_v, *,
                 heads, n_pages, ppb):
    n_pairs, dh = q2.shape
    page = cache_k.shape[1]
    nsp = MOBA_TOPK * ppb
    full = pl.BlockSpec((n_pairs, dh), lambda i, pt, ix, ok: (0, 0))
    hbm = pl.BlockSpec(memory_space=pl.ANY)
    return pl.pallas_call(
        functools.partial(_moba_sample_kernel, heads=heads, n_pages=n_pages, ppb=ppb,
                          scale=dh ** -0.5),
        grid_spec=pltpu.PrefetchScalarGridSpec(
            num_scalar_prefetch=3,
            grid=(1,),
            in_specs=[full, full, full, hbm, hbm],
            out_specs=full,
            scratch_shapes=[
                pltpu.VMEM((GATHER_SLOTS, nsp, page, dh), F32),
                pltpu.VMEM((GATHER_SLOTS, nsp, page, dh), F32),
                pltpu.SemaphoreType.DMA((2, GATHER_SLOTS)),
            ],
        ),
        out_shape=jax.ShapeDtypeStruct((n_pairs, dh), F32),
        compiler_params=_params("arbitrary"),
        name="moba_sample",
    )(page_table_flat, idx_flat, ok_flat, q2, kn2, vn2, cache_k, cache_v)


def kernel(x_prompt, x_sample, cache_k, cache_v, state_pool, page_table, g_pool, w_pool, s_pool,
           g_mlp, w_up, w_down, g_kv, w_k, w_v, g_attn, w_q, w_o, g_final):
    b, t, d = x_prompt.shape
    bs, ts, _ = x_sample.shape
    n_phys, page, heads, dh = cache_k.shape
    hd = heads * dh
    depth = w_up.shape[0]
    n_pool = g_pool.shape[0]
    n_pages = page_table.shape[1]
    past_len = n_pages * page
    assert ts == 1 and MOBA_BLOCK % page == 0 and past_len % MOBA_BLOCK == 0
    assert t % MOBA_BLOCK == 0 and t >= POOL_STATE and dh == LANES
    assert past_len // MOBA_BLOCK >= MOBA_TOPK and heads <= LANES
    ppb = MOBA_BLOCK // page

    assert 1 <= n_pool < depth
    n_attn = depth - n_pool
    row = lambda a: a.reshape(1, -1)

    w_pool_b = w_pool.astype(BF16)
    wu_b = [w_up[0].astype(BF16)] + [None] * (depth - 1)
    wd_b = [w_down[0].astype(BF16)] + [None] * (depth - 1)
    side = [[(w_up, l + 1), (w_down, l + 1)] for l in range(depth - 1)]
    side[max(n_pool - 2, 0)] += [(w_k, None), (w_v, None)]
    for j in range(n_attn):
        side[n_pool + j - 1] += [(w_q, j), (w_o, j)]
    w_k_b = w_v_b = None
    w_q_b, w_o_b = [None] * n_attn, [None] * n_attn

    pt_flat = page_table.reshape(-1).astype(jnp.int32)
    kmean = _kmean(pt_flat, cache_k, batch=bs, n_pages=n_pages, ppb=ppb)

    xf = x_prompt.reshape(b * t, d)
    xs = x_sample.reshape(bs, d)
    pool_p, pool_s = [], []
    kp = vp = ks = vs = None
    for l in range(depth):
        if l < n_pool:
            xp, st = _pool_prompt(xf.reshape(b, t, d), row(g_pool[l]), w_pool_b, l, row(s_pool[l]),
                                  tp=512)
            pool_p.append(st)
            xf = xp.reshape(b * t, d)
            xs, st = _pool_sample(xs, state_pool, row(g_pool[l]), w_pool_b, l, row(s_pool[l]),
                                  pos0=past_len)
            pool_s.append(st)
        else:
            j = l - n_pool
            ga = row(g_attn[j])
            q, qs = _norm_proj(xf, ga, [w_q_b[j]], tm=512, tn=1024, extra=xs)
            a = _moba_prompt(q.reshape(b, t, hd), kp, vp, heads=heads)
            idx, ok = _gate_topk(qs.reshape(bs, heads, dh), kmean)
            a_s = _moba_sample(pt_flat, idx.reshape(-1), ok.reshape(-1), qs.reshape(bs * heads, dh),
                               ks.reshape(bs * heads, dh), vs.reshape(bs * heads, dh),
                               cache_k, cache_v, heads=heads, n_pages=n_pages, ppb=ppb)
            xf, xs = _out_proj(a.reshape(b * t, hd), w_o_b[j], xf, tm=1024, tn=1024,
                               extra=(a_s.reshape(bs, hd), xs))

        last = l == depth - 1
        jobs = [] if last else side[l]
        xf, xs, rounded = _mlp(xf, row(g_mlp[l]), row(g_final), wu_b[l], wd_b[l],
                               final_norm=last, tm=512, tf=1024, extra=xs, casts=jobs)
        for (src, lead), w_b in zip(jobs, rounded):
            if src is w_up:
                wu_b[lead] = w_b
            elif src is w_down:
                wd_b[lead] = w_b
            elif src is w_k:
                w_k_b = w_b
            elif src is w_v:
                w_v_b = w_b
            elif src is w_q:
                w_q_b[lead] = w_b
            else:
                w_o_b[lead] = w_b

        if l == n_pool - 1:
            kf, vf, ks, vs = _norm_proj(xf, row(g_kv), [w_k_b, w_v_b], tm=512, tn=512, extra=xs)
            kp, vp = kf.reshape(b, t, hd), vf.reshape(b, t, hd)
    y_prompt = xf.reshape(b, t, d)
    y_sample = xs.reshape(bs, ts, d)

    return (y_prompt, y_sample,
            kp.reshape(b, t, heads, dh), vp.reshape(b, t, heads, dh), jnp.stack(pool_p, axis=0),
            ks.reshape(bs, ts, heads, dh), vs.reshape(bs, ts, heads, dh), jnp.stack(pool_s, axis=0))
```

```python
import functools

import jax
import jax.numpy as jnp
from jax import lax
from jax.experimental import pallas as pl
from jax.experimental.pallas import tpu as pltpu

EPS = 1e-6
POOL_WINDOWS = (2, 4, 8, 16)
POOL_STATE = max(POOL_WINDOWS) - 1
HALO = max(POOL_WINDOWS)
MOBA_BLOCK = 256
MOBA_TOPK = 3

BF16 = jnp.bfloat16
F32 = jnp.float32
NEG_INF = float("-inf")
INF = float("inf")
MASK_BIAS = 2.0 ** 100
LOG2E = 1.4426950408889634
LANES = 128
SUBLANES = 8
BF16_ROWS = 16
VMEM_LIMIT = 60 * 1024 * 1024

_NT = (((1,), (1,)), ((), ()))


def _params(*semantics):
    return pltpu.CompilerParams(dimension_semantics=semantics, vmem_limit_bytes=VMEM_LIMIT)


def _rms(x, g):
    ms = jnp.mean(x * x, axis=-1, keepdims=True)
    return x * lax.rsqrt(ms + EPS) * g


def _rms_rows(x_ref, g, store, start=0, stop=None):
    n = x_ref.shape[0] if stop is None else stop
    step = min(n - start, BF16_ROWS)
    for r0 in range(start, n, step):
        parts = [_rms(x_ref[r:r + SUBLANES, :], g) for r in range(r0, r0 + step, SUBLANES)]
        store(slice(r0, r0 + step), parts[0] if len(parts) == 1 else jnp.concatenate(parts, axis=0))


def _pad_rows_bf16(x):
    pad = jnp.zeros((BF16_ROWS - x.shape[0], x.shape[1]), F32)
    return jnp.concatenate([x, pad], axis=0).astype(BF16)


MLP_COPY_CHUNKS = 8


def _mlp_kernel(x_ref, g_ref, gf_ref, wu_ref, wd_ref, *refs, final_norm, n_extra, n_cast):
    refs = list(refs)
    xe_ref = refs.pop(0) if n_extra else None
    cast_in = [refs.pop(0) for _ in range(n_cast)]
    o_ref = refs.pop(0)
    oe_ref = refs.pop(0) if n_extra else None
    cast_out = [refs.pop(0) for _ in range(n_cast)]
    h_ref, acc_ref, x_sem, o_sem = refs
    i, j = pl.program_id(0), pl.program_id(1)
    last_j = pl.num_programs(1) - 1
    tm, d = acc_ref.shape
    g = g_ref[...]
    n_chunks = x_sem.shape[0]
    rows_c = tm // n_chunks
    cols_c = d // n_chunks
    base = pl.multiple_of(i * tm, tm)

    @pl.when(j == 0)
    def _():
        copies = [
            pltpu.make_async_copy(x_ref.at[pl.ds(base + c * rows_c, rows_c), :],
                                  acc_ref.at[pl.ds(c * rows_c, rows_c), :], x_sem.at[c])
            for c in range(n_chunks)
        ]
        for cp in copies:
            cp.start()

        def store(rws, h):
            h_ref[rws, :] = h.astype(BF16)
        for c, cp in enumerate(copies):
            cp.wait()
            _rms_rows(acc_ref, g, store, c * rows_c, (c + 1) * rows_c)

    if n_extra:
        @pl.when((i == 0) & (j == 0))
        def _():
            xe = xe_ref[...]
            h_ref[tm:, :] = _pad_rows_bf16(_rms(xe, g))
            oe_ref[...] = xe

    def hidden(rows):
        for src, dst in zip(cast_in, cast_out):
            dst[...] = src[...].astype(BF16)
        u = jnp.dot(h_ref[:rows, :], wu_ref[...], preferred_element_type=F32)
        u = jnp.maximum(u, 0.0)
        return (u * u).astype(BF16)

    def step(rows):
        r = jnp.dot(hidden(rows), wd_ref[...], preferred_element_type=F32)
        acc_ref[...] += r[:tm]
        if rows > tm:
            oe_ref[...] += r[tm:tm + n_extra]

    def write_cols(c):
        return pltpu.make_async_copy(acc_ref.at[:, pl.ds(c * cols_c, cols_c)],
                                     o_ref.at[pl.ds(base, tm), pl.ds(c * cols_c, cols_c)], o_sem.at[c])

    def write_rows(c):
        return pltpu.make_async_copy(acc_ref.at[pl.ds(c * rows_c, rows_c), :],
                                     o_ref.at[pl.ds(base + c * rows_c, rows_c), :], o_sem.at[c])

    def last_step(rows):
        u = hidden(rows)
        for c in range(n_chunks):
            cs = slice(c * cols_c, (c + 1) * cols_c)
            r = jnp.dot(u, wd_ref[:, cs], preferred_element_type=F32)
            acc_ref[:, cs] += r[:tm]
            if rows > tm:
                oe_ref[:, cs] += r[tm:tm + n_extra]
            if not final_norm:
                write_cols(c).start()
        if final_norm:
            gf = gf_ref[...]

            def store(rws, y):
                acc_ref[rws, :] = y
            for c in range(n_chunks):
                _rms_rows(acc_ref, gf, store, c * rows_c, (c + 1) * rows_c)
                write_rows(c).start()
            if rows > tm:
                oe_ref[...] = _rms(oe_ref[...], gf)
        for c in range(n_chunks):
            (write_rows(c) if final_norm else write_cols(c)).wait()

    if n_extra:
        pl.when((i == 0) & (j < last_j))(lambda: step(tm + BF16_ROWS))
        pl.when((i > 0) & (j < last_j))(lambda: step(tm))
        pl.when((i == 0) & (j == last_j))(lambda: last_step(tm + BF16_ROWS))
        pl.when((i > 0) & (j == last_j))(lambda: last_step(tm))
    else:
        pl.when(j < last_j)(lambda: step(tm))
        pl.when(j == last_j)(lambda: last_step(tm))


def _slab_plan(rows, cols, steps):
    best = None
    for cs in (1, 2, 4, 8):
        if cols % (cs * LANES):
            continue
        for rb in range(BF16_ROWS, rows + 1, BF16_ROWS):
            if rows % rb == 0 and (rows // rb) * cs <= steps:
                if best is None or rb * (cols // cs) < best[0] * (cols // best[1]):
                    best = (rb, cs)
                break
    assert best is not None, (rows, cols, steps)
    return best


def _slab_spec(rows, cols, steps, nj, lead):
    rb, cs = _slab_plan(rows, cols, steps)
    last = (rows // rb) * cs - 1

    def index(i, j):
        s = jnp.minimum(i * nj + j, last)
        return s // cs, s % cs

    if lead is None:
        spec_in = pl.BlockSpec((rb, cols // cs), index)
    else:
        spec_in = pl.BlockSpec((None, rb, cols // cs), lambda i, j: (lead,) + index(i, j))
    return spec_in, pl.BlockSpec((rb, cols // cs), index)


def _mlp(x, g, gf, w_up, w_down, *, final_norm, tm, tf, extra=None, casts=()):
    m, d = x.shape
    f = w_up.shape[1]
    tm = min(tm, m)
    tf = min(tf, f)
    ni, nj = m // tm, f // tf
    n_extra = 0 if extra is None else extra.shape[0]
    assert n_extra % SUBLANES == 0 and n_extra <= BF16_ROWS
    const = lambda i, j: (0, 0)
    in_specs = [
        pl.BlockSpec(memory_space=pl.ANY),
        pl.BlockSpec((1, d), const),
        pl.BlockSpec((1, d), const),
        pl.BlockSpec((d, tf), lambda i, j: (0, j)),
        pl.BlockSpec((tf, d), lambda i, j: (j, 0)),
    ]
    out_specs = [pl.BlockSpec(memory_space=pl.ANY)]
    out_shape = [jax.ShapeDtypeStruct((m, d), F32)]
    chunks = MLP_COPY_CHUNKS
    if tm % (chunks * BF16_ROWS) or d % (chunks * LANES):
        chunks = 1
    args = [x, g, gf, w_up, w_down]
    if n_extra:
        in_specs.append(pl.BlockSpec((n_extra, d), const))
        out_specs.append(pl.BlockSpec((n_extra, d), const))
        out_shape.append(jax.ShapeDtypeStruct((n_extra, d), F32))
        args.append(extra)
    for arr, lead in casts:
        rows, cols = arr.shape[-2:]
        spec_in, spec_out = _slab_spec(rows, cols, ni * nj, nj, lead)
        in_specs.append(spec_in)
        out_specs.append(spec_out)
        out_shape.append(jax.ShapeDtypeStruct((rows, cols), BF16))
        args.append(arr)
    outs = pl.pallas_call(
        functools.partial(_mlp_kernel, final_norm=final_norm, n_extra=n_extra, n_cast=len(casts)),
        grid=(ni, nj),
        in_specs=in_specs,
        out_specs=out_specs,
        out_shape=out_shape,
        scratch_shapes=[pltpu.VMEM((tm + (BF16_ROWS if n_extra else 0), d), BF16),
                        pltpu.VMEM((tm, d), F32),
                        pltpu.SemaphoreType.DMA((chunks,)),
                        pltpu.SemaphoreType.DMA((chunks,))],
        compiler_params=_params("arbitrary", "arbitrary"),
        name="mlp",
    )(*args)
    k = 2 if n_extra else 1
    return outs[0], (outs[1] if n_extra else None), list(outs[k:])


def _extra_col_index(nj):
    return lambda i, j: (0, jnp.where(i == 0, j, nj - 1))


def _norm_proj_kernel(x_ref, g_ref, *refs, n, n_extra):
    refs = list(refs)
    xe_ref = refs.pop(0) if n_extra else None
    w_refs = [refs.pop(0) for _ in range(n)]
    o_refs = [refs.pop(0) for _ in range(n)]
    oe_refs = [refs.pop(0) for _ in range(n)] if n_extra else []
    (h_ref,) = refs
    i, j = pl.program_id(0), pl.program_id(1)
    tm = x_ref.shape[0]
    g = g_ref[...]

    @pl.when(j == 0)
    def _():
        def store(rows, h):
            h_ref[rows, :] = h.astype(BF16)
        _rms_rows(x_ref, g, store)

    if n_extra:
        @pl.when((i == 0) & (j == 0))
        def _():
            h_ref[tm:, :] = _pad_rows_bf16(_rms(xe_ref[...], g))

    h = h_ref[:tm, :]
    for w_ref, o_ref in zip(w_refs, o_refs):
        o_ref[...] = jnp.dot(h, w_ref[...], preferred_element_type=F32)

    if n_extra:
        @pl.when(i == 0)
        def _():
            he = h_ref[tm:, :]
            for w_ref, oe_ref in zip(w_refs, oe_refs):
                oe_ref[...] = jnp.dot(he, w_ref[...], preferred_element_type=F32)[:n_extra]


def _norm_proj(x, g, weights, *, tm, tn, extra=None):
    m, d = x.shape
    nout = weights[0].shape[1]
    tm = min(tm, m)
    tn = min(tn, nout)
    n = len(weights)
    ni, nj = m // tm, nout // tn
    n_extra = 0 if extra is None else extra.shape[0]
    assert n_extra % SUBLANES == 0 and n_extra <= BF16_ROWS
    in_specs = [pl.BlockSpec((tm, d), lambda i, j: (i, 0)), pl.BlockSpec((1, d), lambda i, j: (0, 0))]
    out_specs = [pl.BlockSpec((tm, tn), lambda i, j: (i, j))] * n
    out_shape = [jax.ShapeDtypeStruct((m, nout), F32)] * n
    args = [x, g]
    if n_extra:
        in_specs.append(pl.BlockSpec((n_extra, d), lambda i, j: (0, 0)))
        out_specs += [pl.BlockSpec((n_extra, tn), _extra_col_index(nj))] * n
        out_shape += [jax.ShapeDtypeStruct((n_extra, nout), F32)] * n
        args.append(extra)
    in_specs += [pl.BlockSpec((d, tn), lambda i, j: (0, j))] * n
    return pl.pallas_call(
        functools.partial(_norm_proj_kernel, n=n, n_extra=n_extra),
        grid=(ni, nj),
        in_specs=in_specs,
        out_specs=out_specs,
        out_shape=out_shape,
        scratch_shapes=[pltpu.VMEM((tm + (BF16_ROWS if n_extra else 0), d), BF16)],
        compiler_params=_params("arbitrary", "arbitrary"),
        name="norm_proj",
    )(*args, *weights)


def _out_proj_kernel(a_ref, w_ref, x_ref, *refs, n_extra):
    if n_extra:
        ae_ref, xe_ref, o_ref, oe_ref = refs
    else:
        (o_ref,) = refs
    a = a_ref[...].astype(BF16)
    o_ref[...] = x_ref[...] + jnp.dot(a, w_ref[...], preferred_element_type=F32)

    if n_extra:
        @pl.when(pl.program_id(0) == 0)
        def _():
            r = jnp.dot(_pad_rows_bf16(ae_ref[...]), w_ref[...], preferred_element_type=F32)
            oe_ref[...] = xe_ref[...] + r[:n_extra]


def _out_proj(a, w_o, x, *, tm, tn, extra=None):
    m, d = x.shape
    k = a.shape[1]
    tm = min(tm, m)
    tn = min(tn, d)
    ni, nj = m // tm, d // tn
    n_extra = 0 if extra is None else extra[0].shape[0]
    assert n_extra % SUBLANES == 0 and n_extra <= BF16_ROWS
    in_specs = [
        pl.BlockSpec((tm, k), lambda i, j: (i, 0)),
        pl.BlockSpec((k, tn), lambda i, j: (0, j)),
        pl.BlockSpec((tm, tn), lambda i, j: (i, j)),
    ]
    out_specs = [pl.BlockSpec((tm, tn), lambda i, j: (i, j))]
    out_shape = [jax.ShapeDtypeStruct((m, d), F32)]
    args = [a, w_o, x]
    if n_extra:
        in_specs += [pl.BlockSpec((n_extra, k), lambda i, j: (0, 0)),
                     pl.BlockSpec((n_extra, tn), _extra_col_index(nj))]
        out_specs.append(pl.BlockSpec((n_extra, tn), _extra_col_index(nj)))
        out_shape.append(jax.ShapeDtypeStruct((n_extra, d), F32))
        args += list(extra)
    outs = pl.pallas_call(
        functools.partial(_out_proj_kernel, n_extra=n_extra),
        grid=(ni, nj),
        in_specs=in_specs,
        out_specs=out_specs,
        out_shape=out_shape,
        compiler_params=_params("arbitrary", "arbitrary"),
        name="out_proj",
    )(*args)
    return outs[0], (outs[1] if n_extra else None)


def _window_sums(ext, w, rows):
    s, span = ext, 1
    while span < w:
        s = s[span:] + s[:-span]
        span *= 2
    start = HALO + 1 - w
    return s[start:start + rows]


def _pool_prompt_kernel(x_ref, halo_ref, g_ref, w_ref, s_ref, o_ref, st_ref, *, tp):
    i = pl.program_id(1)
    x = x_ref[0]
    g = g_ref[...]
    h = _rms(x, g)
    hh = _rms(halo_ref[0], g)
    hh = jnp.where(i > 0, hh, 0.0)
    ext = jnp.concatenate([hh, h], axis=0)
    cg = x.shape[1] // len(POOL_WINDOWS)
    pos = i * tp + lax.broadcasted_iota(jnp.int32, (tp, 1), 0)
    for gi, w in enumerate(POOL_WINDOWS):
        c0, c1 = gi * cg, (gi + 1) * cg
        wsum = _window_sums(ext[:, c0:c1], w, tp)
        cnt = jnp.minimum(pos + 1, w).astype(F32)
        dlt = wsum * (1.0 / cnt) - h[:, c0:c1]
        y = jnp.dot(dlt.astype(BF16), w_ref[gi], preferred_element_type=F32)
        o_ref[0, :, c0:c1] = x[:, c0:c1] + y * s_ref[:, c0:c1]

    @pl.when(i == pl.num_programs(1) - 1)
    def _():
        st_ref[0] = h[tp - POOL_STATE:, :]


def _pool_prompt(x, g, w_pool, layer, s, *, tp):
    b, t, d = x.shape
    tp = min(tp, t)
    cg = d // len(POOL_WINDOWS)
    hb = tp // HALO
    return pl.pallas_call(
        functools.partial(_pool_prompt_kernel, tp=tp),
        grid=(b, t // tp),
        in_specs=[
            pl.BlockSpec((1, tp, d), lambda bi, i: (bi, i, 0)),
            pl.BlockSpec((1, HALO, d), lambda bi, i: (bi, jnp.maximum(i * hb - 1, 0), 0)),
            pl.BlockSpec((1, d), lambda bi, i: (0, 0)),
            pl.BlockSpec((None, len(POOL_WINDOWS), cg, cg), lambda bi, i: (layer, 0, 0, 0)),
            pl.BlockSpec((1, d), lambda bi, i: (0, 0)),
        ],
        out_specs=[
            pl.BlockSpec((1, tp, d), lambda bi, i: (bi, i, 0)),
            pl.BlockSpec((1, POOL_STATE, d), lambda bi, i: (bi, 0, 0)),
        ],
        out_shape=[jax.ShapeDtypeStruct((b, t, d), F32),
                   jax.ShapeDtypeStruct((b, POOL_STATE, d), F32)],
        compiler_params=_params("parallel", "arbitrary"),
        name="pool_prompt",
    )(x, x, g, w_pool, s)


def _pool_sample_kernel(x_ref, prev_ref, g_ref, w_ref, s_ref, o_ref, st_ref, d_ref, *, pos0):
    x = x_ref[...]
    h = _rms(x, g_ref[...])
    nb = x.shape[0]
    cg = x.shape[1] // len(POOL_WINDOWS)
    for b in range(nb):
        pv = prev_ref[b]
        hb = h[b:b + 1]
        st_ref[b, 0:POOL_STATE - 1, :] = pv[1:POOL_STATE]
        st_ref[b, POOL_STATE - 1:POOL_STATE, :] = hb
        for gi, w in enumerate(POOL_WINDOWS):
            c0, c1 = gi * cg, (gi + 1) * cg
            wsum = hb[:, c0:c1] + jnp.sum(pv[POOL_STATE + 1 - w:, c0:c1], axis=0, keepdims=True)
            cnt = float(min(pos0 + 1, w))
            d_ref[b:b + 1, c0:c1] = wsum / cnt - hb[:, c0:c1]
    for gi in range(len(POOL_WINDOWS)):
        c0, c1 = gi * cg, (gi + 1) * cg
        y = jnp.dot(d_ref[:, c0:c1].astype(BF16), w_ref[gi], preferred_element_type=F32)
        o_ref[:, c0:c1] = x[:, c0:c1] + y * s_ref[:, c0:c1]


def _pool_sample(x, prev, g, w_pool, layer, s, *, pos0):
    b, d = x.shape
    cg = d // len(POOL_WINDOWS)
    return pl.pallas_call(
        functools.partial(_pool_sample_kernel, pos0=pos0),
        grid=(1,),
        in_specs=[
            pl.BlockSpec((b, d), lambda i: (0, 0)),
            pl.BlockSpec((None, b, POOL_STATE, d), lambda i: (layer, 0, 0, 0)),
            pl.BlockSpec((1, d), lambda i: (0, 0)),
            pl.BlockSpec((None, len(POOL_WINDOWS), cg, cg), lambda i: (layer, 0, 0, 0)),
            pl.BlockSpec((1, d), lambda i: (0, 0)),
        ],
        out_specs=[
            pl.BlockSpec((b, d), lambda i: (0, 0)),
            pl.BlockSpec((b, POOL_STATE, d), lambda i: (0, 0, 0)),
        ],
        out_shape=[jax.ShapeDtypeStruct((b, d), F32),
                   jax.ShapeDtypeStruct((b, POOL_STATE, d), F32)],
        scratch_shapes=[pltpu.VMEM((b, d), F32)],
        compiler_params=_params("arbitrary"),
        name="pool_sample",
    )(x, prev, g, w_pool, s)


ROW_CHUNK = 64


def _moba_prompt_kernel(q_ref, k_ref, v_ref, o_ref, qa_ref, ka_ref, va_ref, s_ref, p_ref, *, scale):
    t, dh = q_ref.shape[1], q_ref.shape[2]
    blk = MOBA_BLOCK
    nb = t // blk
    nbp = -(-nb // SUBLANES) * SUBLANES
    cexp = scale * LOG2E

    k = k_ref[0]
    lane_t = lax.broadcasted_iota(jnp.int32, (t, LANES), 1)
    ka_ref[:dh, :] = k.T.astype(BF16)
    mask_row = lax.broadcasted_iota(jnp.int32, (LANES, t), 0)
    mask_key = lax.broadcasted_iota(jnp.int32, (LANES, t), 1) // blk
    ka_ref[dh:, :] = jnp.where(mask_row == mask_key, -MASK_BIAS, 0.0).astype(BF16)
    va_ref[:, :dh] = v_ref[0].astype(BF16)
    va_ref[:, dh:] = jnp.where(lane_t == 0, 1.0, 0.0).astype(BF16)
    means = [jnp.mean(k[n * blk:(n + 1) * blk], axis=0, keepdims=True) for n in range(nb)]
    if nbp > nb:
        means.append(jnp.zeros((nbp - nb, dh), F32))
    kmean = jnp.concatenate(means, axis=0)

    km_hi = kmean.astype(BF16)
    km_lo = kmean - km_hi.astype(F32)
    km_parts = jnp.concatenate([km_hi.astype(F32), km_lo], axis=0).astype(BF16)

    n_iota = lax.broadcasted_iota(jnp.int32, (nbp, blk), 0)
    lane_q = lax.broadcasted_iota(jnp.int32, (blk, LANES), 1)
    row_c = lax.broadcasted_iota(jnp.int32, (ROW_CHUNK, blk), 0)
    col_c = lax.broadcasted_iota(jnp.int32, (ROW_CHUNK, blk), 1)

    def augment(c):
        buf = c
        qc = q_ref[0, c * blk:(c + 1) * blk, :]
        q_hi = qc.astype(BF16)
        qa_ref[buf, :, :dh] = q_hi
        if c > 0:
            q_lo = (qc - q_hi.astype(F32)).astype(BF16)
            g_hi = lax.dot_general(km_parts, q_hi, _NT, preferred_element_type=F32)
            g_lo = lax.dot_general(km_parts, q_lo, _NT, preferred_element_type=F32)
            gt = g_hi[:nbp] + g_hi[nbp:] + g_lo[:nbp]
            past = n_iota < c
            gt = jnp.where(past, gt, NEG_INF)
            rank = jnp.zeros((nbp, blk), jnp.int32)
            for m in range(c):
                rm = gt[m:m + 1, :]
                beats = (rm > gt) | ((rm == gt) & (n_iota > m))
                rank = rank + beats.astype(jnp.int32)
            sel_t = past & (rank < MOBA_TOPK) & (jnp.abs(gt) < INF)
            sel_t = jnp.concatenate(
                [sel_t.astype(F32), jnp.zeros((LANES - nbp, blk), F32)], axis=0)
            sel = sel_t.T
            qa_ref[buf, :, dh:] = jnp.where(lane_q < c, 1.0 - sel, 0.0).astype(BF16)
        else:
            qa_ref[buf, :, dh:] = jnp.zeros((blk, LANES), BF16)

    def scores(c):
        nk = (c + 1) * blk
        s_ref[c % 2, :, :nk] = jnp.dot(qa_ref[c], ka_ref[:, :nk], preferred_element_type=F32)

    def weighted_values(c):
        nk = (c + 1) * blk
        o = jnp.dot(p_ref[c % 2, :, :nk], va_ref[:nk, :], preferred_element_type=F32)
        o_ref[0, c * blk:(c + 1) * blk, :] = (o[:, :dh] * (1.0 / o[:, dh:dh + 1])).astype(o_ref.dtype)

    for c in range(nb):
        augment(c)
    scores(0)
    for c in range(nb):
        buf = c % 2
        nk = (c + 1) * blk
        if c + 1 < nb:
            scores(c + 1)
        if c > 0:
            weighted_values(c - 1)
        for r in range(blk // ROW_CHUNK):
            r0 = r * ROW_CHUNK
            causal = col_c <= row_c + r0
            sd = jnp.where(causal, s_ref[buf, r0:r0 + ROW_CHUNK, c * blk:nk], -MASK_BIAS)
            mx = sd
            for n in range(c):
                mx = jnp.maximum(mx, s_ref[buf, r0:r0 + ROW_CHUNK, n * blk:(n + 1) * blk])
            mb = jnp.broadcast_to(jnp.max(mx, axis=1, keepdims=True), (ROW_CHUNK, blk))
            p_ref[buf, r0:r0 + ROW_CHUNK, c * blk:nk] = jnp.exp2((sd - mb) * cexp).astype(BF16)
            for n in range(c):
                sn = s_ref[buf, r0:r0 + ROW_CHUNK, n * blk:(n + 1) * blk]
                p_ref[buf, r0:r0 + ROW_CHUNK, n * blk:(n + 1) * blk] = (
                    jnp.exp2((sn - mb) * cexp).astype(BF16))
    weighted_values(nb - 1)


def _moba_prompt(q, k, v, *, heads):
    b, t, hd = q.shape
    dh = hd // heads
    spec = pl.BlockSpec((1, t, dh), lambda bi, h: (bi, 0, h))
    return pl.pallas_call(
        functools.partial(_moba_prompt_kernel, scale=dh ** -0.5),
        grid=(b, heads),
        in_specs=[spec, spec, spec],
        out_specs=spec,
        out_shape=jax.ShapeDtypeStruct((b, t, hd), BF16),
        scratch_shapes=[
            pltpu.VMEM((t // MOBA_BLOCK, MOBA_BLOCK, dh + LANES), BF16),
            pltpu.VMEM((dh + LANES, t), BF16),
            pltpu.VMEM((t, dh + LANES), BF16),
            pltpu.VMEM((2, MOBA_BLOCK, t), F32),
            pltpu.VMEM((2, MOBA_BLOCK, t), BF16),
        ],
        compiler_params=_params("parallel", "parallel"),
        name="moba_prompt",
    )(q, k, v)


def _kmean_kernel(pt_ref, *refs, ppb):
    k_refs, o_ref = refs[:ppb], refs[ppb]
    n = pl.program_id(1)
    acc = jnp.sum(k_refs[0][0], axis=0)
    for r in k_refs[1:]:
        acc = acc + jnp.sum(r[0], axis=0)
    rows = ppb * k_refs[0].shape[1]
    o_ref[0, pl.ds(n, 1)] = (acc * (1.0 / rows))[None]


def _kmean(page_table_flat, cache_k, *, batch, n_pages, ppb):
    _, page, heads, dh = cache_k.shape
    nbp = n_pages // ppb
    specs = [
        pl.BlockSpec((1, page, heads, dh),
                     lambda b, n, pt, p=p: (pt[b * n_pages + n * ppb + p], 0, 0, 0))
        for p in range(ppb)
    ]
    return pl.pallas_call(
        functools.partial(_kmean_kernel, ppb=ppb),
        grid_spec=pltpu.PrefetchScalarGridSpec(
            num_scalar_prefetch=1,
            grid=(batch, nbp),
            in_specs=specs,
            out_specs=pl.BlockSpec((1, nbp, heads, dh), lambda b, n, pt: (b, 0, 0, 0)),
        ),
        out_shape=jax.ShapeDtypeStruct((batch, nbp, heads, dh), F32),
        compiler_params=_params("parallel", "arbitrary"),
        name="kmean",
    )(page_table_flat, *([cache_k] * ppb))


def _gate_topk_kernel(q_ref, km_ref, idx_ref, ok_ref):
    nbt = q_ref.shape[0]
    nbp, heads = km_ref.shape[1], km_ref.shape[2]
    n_iota = lax.broadcasted_iota(jnp.int32, (nbp, heads, 1), 0)
    for b in range(nbt):
        g = jnp.sum(km_ref[b] * q_ref[b][None], axis=-1, keepdims=True)
        for s in range(MOBA_TOPK):
            mx = jnp.max(g, axis=0)
            idx = jnp.min(jnp.where(g == mx[None], n_iota, nbp), axis=0)
            idx = jnp.minimum(idx, nbp - 1)
            idx_ref[b, s] = idx
            ok_ref[b, s] = (jnp.abs(mx) < INF).astype(jnp.int32)
            g = jnp.where(n_iota == idx[None], NEG_INF, g)


def _gate_topk(q3, kmean):
    b, heads, dh = q3.shape
    nbp = kmean.shape[1]
    out = jax.ShapeDtypeStruct((b, MOBA_TOPK, heads, 1), jnp.int32)
    out_spec = pl.BlockSpec((b, MOBA_TOPK, heads, 1), lambda i: (0, 0, 0, 0))
    return pl.pallas_call(
        _gate_topk_kernel,
        grid=(1,),
        in_specs=[pl.BlockSpec((b, heads, dh), lambda i: (0, 0, 0)),
                  pl.BlockSpec((b, nbp, heads, dh), lambda i: (0, 0, 0, 0))],
        out_specs=[out_spec, out_spec],
        out_shape=[out, out],
        compiler_params=_params("arbitrary"),
        name="gate_topk",
    )(q3, kmean)


GATHER_SLOTS = 8


def _moba_sample_kernel(pt_ref, idx_ref, ok_ref, q_ref, kn_ref, vn_ref, ck_ref, cv_ref, o_ref,
                        kbuf, vbuf, sem, *, heads, n_pages, ppb, scale):
    n_pairs = q_ref.shape[0]
    nsp, page, dh = kbuf.shape[1], kbuf.shape[2], kbuf.shape[3]
    rows = nsp * page

    def gather(pair, slot):
        b, h = pair // heads, pair % heads
        cps = []
        for s in range(MOBA_TOPK):
            blk = idx_ref[(b * MOBA_TOPK + s) * heads + h]
            for p in range(ppb):
                phys = pt_ref[b * n_pages + blk * ppb + p]
                j = s * ppb + p
                cps.append(pltpu.make_async_copy(ck_ref.at[phys, :, h, :], kbuf.at[slot, j], sem.at[0, slot]))
                cps.append(pltpu.make_async_copy(cv_ref.at[phys, :, h, :], vbuf.at[slot, j], sem.at[1, slot]))
        return cps

    n_slots = kbuf.shape[0]
    ahead = n_slots - 1
    for t in range(min(ahead, n_pairs)):
        for cp in gather(t, t):
            cp.start()
    row_blk = lax.broadcasted_iota(jnp.int32, (rows, 1), 0) // MOBA_BLOCK

    def body(pair, carry):
        slot = pair % n_slots

        @pl.when(pair + ahead < n_pairs)
        def _():
            for cp in gather(pair + ahead, (pair + ahead) % n_slots):
                cp.start()

        for cp in gather(pair, slot):
            cp.wait()
        b, h = pair // heads, pair % heads
        kk = kbuf[slot].reshape(rows, dh)
        vv = vbuf[slot].reshape(rows, dh)
        q = q_ref[pl.ds(pair, 1), :]
        s = jnp.sum(kk * q, axis=1, keepdims=True) * scale
        for j in range(MOBA_TOPK):
            ok = ok_ref[(b * MOBA_TOPK + j) * heads + h]
            s = s + jnp.where(row_blk == j, jnp.where(ok == 0, NEG_INF, 0.0), 0.0)
        s_own = jnp.sum(q * kn_ref[pl.ds(pair, 1), :], axis=1, keepdims=True) * scale
        mx = jnp.maximum(jnp.max(s, axis=0, keepdims=True), s_own)
        p = jnp.exp(s - mx)
        p_own = jnp.exp(s_own - mx)
        l = jnp.sum(p, axis=0, keepdims=True) + p_own
        o = jnp.sum(p * vv, axis=0, keepdims=True) + p_own * vn_ref[pl.ds(pair, 1), :]
        o_ref[pl.ds(pair, 1), :] = o / l
        return carry

    lax.fori_loop(0, n_pairs, body, 0)


def _moba_sample(page_table_flat, idx_flat, ok_flat, q2, kn2, vn2, cache_k, cache_v, *,
                 heads, n_pages, ppb):
    n_pairs, dh = q2.shape
    page = cache_k.shape[1]
    nsp = MOBA_TOPK * ppb
    full = pl.BlockSpec((n_pairs, dh), lambda i, pt, ix, ok: (0, 0))
    hbm = pl.BlockSpec(memory_space=pl.ANY)
    return pl.pallas_call(
        functools.partial(_moba_sample_kernel, heads=heads, n_pages=n_pages, ppb=ppb,
                          scale=dh ** -0.5),
        grid_spec=pltpu.PrefetchScalarGridSpec(
            num_scalar_prefetch=3,
            grid=(1,),
            in_specs=[full, full, full, hbm, hbm],
            out_specs=full,
            scratch_shapes=[
                pltpu.VMEM((GATHER_SLOTS, nsp, page, dh), F32),
                pltpu.VMEM((GATHER_SLOTS, nsp, page, dh), F32),
                pltpu.SemaphoreType.DMA((2, GATHER_SLOTS)),
            ],
        ),
        out_shape=jax.ShapeDtypeStruct((n_pairs, dh), F32),
        compiler_params=_params("arbitrary"),
        name="moba_sample",
    )(page_table_flat, idx_flat, ok_flat, q2, kn2, vn2, cache_k, cache_v)


def kernel(x_prompt, x_sample, cache_k, cache_v, state_pool, page_table, g_pool, w_pool, s_pool,
           g_mlp, w_up, w_down, g_kv, w_k, w_v, g_attn, w_q, w_o, g_final):
    b, t, d = x_prompt.shape
    bs, ts, _ = x_sample.shape
    n_phys, page, heads, dh = cache_k.shape
    hd = heads * dh
    depth = w_up.shape[0]
    n_pool = g_pool.shape[0]
    n_pages = page_table.shape[1]
    past_len = n_pages * page
    assert ts == 1 and MOBA_BLOCK % page == 0 and past_len % MOBA_BLOCK == 0
    assert t % MOBA_BLOCK == 0 and t >= POOL_STATE and dh == LANES
    assert past_len // MOBA_BLOCK >= MOBA_TOPK and heads <= LANES
    ppb = MOBA_BLOCK // page

    assert 1 <= n_pool < depth
    n_attn = depth - n_pool
    row = lambda a: a.reshape(1, -1)

    w_pool_b = w_pool.astype(BF16)
    wu_b = [w_up[0].astype(BF16)] + [None] * (depth - 1)
    wd_b = [w_down[0].astype(BF16)] + [None] * (depth - 1)
    side = [[(w_up, l + 1), (w_down, l + 1)] for l in range(depth - 1)]
    side[max(n_pool - 2, 0)] += [(w_k, None), (w_v, None)]
    for j in range(n_attn):
        side[n_pool + j - 1] += [(w_q, j), (w_o, j)]
    w_k_b = w_v_b = None
    w_q_b, w_o_b = [None] * n_attn, [None] * n_attn

    pt_flat = page_table.reshape(-1).astype(jnp.int32)
    kmean = _kmean(pt_flat, cache_k, batch=bs, n_pages=n_pages, ppb=ppb)

    xf = x_prompt.reshape(b * t, d)
    xs = x_sample.reshape(bs, d)
    pool_p, pool_s = [], []
    kp = vp = ks = vs = None
    for l in range(depth):
        if l < n_pool:
            xp, st = _pool_prompt(xf.reshape(b, t, d), row(g_pool[l]), w_pool_b, l, row(s_pool[l]),
                                  tp=512)
            pool_p.append(st)
            xf = xp.reshape(b * t, d)
            xs, st = _pool_sample(xs, state_pool, row(g_pool[l]), w_pool_b, l, row(s_pool[l]),
                                  pos0=past_len)
            pool_s.append(st)
        else:
            j = l - n_pool
            if j == 0:
                q, qs = q_first
            else:
                q, qs = _norm_proj(xf, row(g_attn[j]), [w_q_b[j]], tm=512, tn=1024, extra=xs)
            a = _moba_prompt(q.reshape(b, t, hd), kp, vp, heads=heads)
            idx, ok = _gate_topk(qs.reshape(bs, heads, dh), kmean)
            a_s = _moba_sample(pt_flat, idx.reshape(-1), ok.reshape(-1), qs.reshape(bs * heads, dh),
                               ks.reshape(bs * heads, dh), vs.reshape(bs * heads, dh),
                               cache_k, cache_v, heads=heads, n_pages=n_pages, ppb=ppb)
            xf, xs = _out_proj(a.reshape(b * t, hd), w_o_b[j], xf, tm=1024, tn=1024,
                               extra=(a_s.reshape(bs, hd), xs))

        last = l == depth - 1
        jobs = [] if last else side[l]
        xf, xs, rounded = _mlp(xf, row(g_mlp[l]), row(g_final), wu_b[l], wd_b[l],
                               final_norm=last, tm=512, tf=1024, extra=xs, casts=jobs)
        for (src, lead), w_b in zip(jobs, rounded):
            if src is w_up:
                wu_b[lead] = w_b
            elif src is w_down:
                wd_b[lead] = w_b
            elif src is w_k:
                w_k_b = w_b
            elif src is w_v:
                w_v_b = w_b
            elif src is w_q:
                w_q_b[lead] = w_b
            else:
                w_o_b[lead] = w_b

        if l == n_pool - 1:
            q_first = _norm_proj(xf, row(g_attn[0]), [w_q_b[0]], tm=512, tn=1024, extra=xs)
            kf, vf, ks, vs = _norm_proj(xf, row(g_kv), [w_k_b, w_v_b], tm=512, tn=512, extra=xs)
            kp, vp = kf.reshape(b, t, hd), vf.reshape(b, t, hd)
    y_prompt = xf.reshape(b, t, d)
    y_sample = xs.reshape(bs, ts, d)

    return (y_prompt, y_sample,
            kp.reshape(b, t, heads, dh), vp.reshape(b, t, heads, dh), jnp.stack(pool_p, axis=0),
            ks.reshape(bs, ts, heads, dh), vs.reshape(bs, ts, heads, dh), jnp.stack(pool_s, axis=0))
```

```python
import functools

import jax
import jax.numpy as jnp
from jax import lax
from jax.experimental import pallas as pl
from jax.experimental.pallas import tpu as pltpu

EPS = 1e-6
POOL_WINDOWS = (2, 4, 8, 16)
POOL_STATE = max(POOL_WINDOWS) - 1
HALO = max(POOL_WINDOWS)
MOBA_BLOCK = 256
MOBA_TOPK = 3

BF16 = jnp.bfloat16
F32 = jnp.float32
NEG_INF = float("-inf")
INF = float("inf")
MASK_BIAS = 2.0 ** 100
LOG2E = 1.4426950408889634
LANES = 128
SUBLANES = 8
BF16_ROWS = 16
VMEM_LIMIT = 60 * 1024 * 1024

_NT = (((1,), (1,)), ((), ()))


def _params(*semantics):
    return pltpu.CompilerParams(dimension_semantics=semantics, vmem_limit_bytes=VMEM_LIMIT)


def _rms(x, g):
    ms = jnp.mean(x * x, axis=-1, keepdims=True)
    return x * lax.rsqrt(ms + EPS) * g


def _rms_rows(x_ref, g, store, start=0, stop=None):
    n = x_ref.shape[0] if stop is None else stop
    step = min(n - start, BF16_ROWS)
    for r0 in range(start, n, step):
        parts = [_rms(x_ref[r:r + SUBLANES, :], g) for r in range(r0, r0 + step, SUBLANES)]
        store(slice(r0, r0 + step), parts[0] if len(parts) == 1 else jnp.concatenate(parts, axis=0))


def _pad_rows_bf16(x):
    pad = jnp.zeros((BF16_ROWS - x.shape[0], x.shape[1]), F32)
    return jnp.concatenate([x, pad], axis=0).astype(BF16)


MLP_COPY_CHUNKS = 8


def _mlp_kernel(x_ref, g_ref, gf_ref, wu_ref, wd_ref, *refs, final_norm, n_extra, n_cast):
    refs = list(refs)
    xe_ref = refs.pop(0) if n_extra else None
    cast_in = [refs.pop(0) for _ in range(n_cast)]
    o_ref = refs.pop(0)
    oe_ref = refs.pop(0) if n_extra else None
    cast_out = [refs.pop(0) for _ in range(n_cast)]
    h_ref, acc_ref, x_sem, o_sem = refs
    i, j = pl.program_id(0), pl.program_id(1)
    last_j = pl.num_programs(1) - 1
    tm, d = acc_ref.shape
    g = g_ref[...]
    n_chunks = x_sem.shape[0]
    rows_c = tm // n_chunks
    cols_c = d // n_chunks
    base = pl.multiple_of(i * tm, tm)

    @pl.when(j == 0)
    def _():
        copies = [
            pltpu.make_async_copy(x_ref.at[pl.ds(base + c * rows_c, rows_c), :],
                                  acc_ref.at[pl.ds(c * rows_c, rows_c), :], x_sem.at[c])
            for c in range(n_chunks)
        ]
        for cp in copies:
            cp.start()

        def store(rws, h):
            h_ref[rws, :] = h.astype(BF16)
        for c, cp in enumerate(copies):
            cp.wait()
            _rms_rows(acc_ref, g, store, c * rows_c, (c + 1) * rows_c)

    if n_extra:
        @pl.when((i == 0) & (j == 0))
        def _():
            xe = xe_ref[...]
            h_ref[tm:, :] = _pad_rows_bf16(_rms(xe, g))
            oe_ref[...] = xe

    def hidden(rows):
        for src, dst in zip(cast_in, cast_out):
            dst[...] = src[...].astype(BF16)
        u = jnp.dot(h_ref[:rows, :], wu_ref[...], preferred_element_type=F32)
        u = jnp.maximum(u, 0.0)
        return (u * u).astype(BF16)

    def step(rows):
        r = jnp.dot(hidden(rows), wd_ref[...], preferred_element_type=F32)
        acc_ref[...] += r[:tm]
        if rows > tm:
            oe_ref[...] += r[tm:tm + n_extra]

    def write_cols(c):
        return pltpu.make_async_copy(acc_ref.at[:, pl.ds(c * cols_c, cols_c)],
                                     o_ref.at[pl.ds(base, tm), pl.ds(c * cols_c, cols_c)], o_sem.at[c])

    def write_rows(c):
        return pltpu.make_async_copy(acc_ref.at[pl.ds(c * rows_c, rows_c), :],
                                     o_ref.at[pl.ds(base + c * rows_c, rows_c), :], o_sem.at[c])

    def last_step(rows):
        u = hidden(rows)
        for c in range(n_chunks):
            cs = slice(c * cols_c, (c + 1) * cols_c)
            r = jnp.dot(u, wd_ref[:, cs], preferred_element_type=F32)
            acc_ref[:, cs] += r[:tm]
            if rows > tm:
                oe_ref[:, cs] += r[tm:tm + n_extra]
            if not final_norm:
                write_cols(c).start()
        if final_norm:
            gf = gf_ref[...]

            def store(rws, y):
                acc_ref[rws, :] = y
            for c in range(n_chunks):
                _rms_rows(acc_ref, gf, store, c * rows_c, (c + 1) * rows_c)
                write_rows(c).start()
            if rows > tm:
                oe_ref[...] = _rms(oe_ref[...], gf)
        for c in range(n_chunks):
            (write_rows(c) if final_norm else write_cols(c)).wait()

    if n_extra:
        pl.when((i == 0) & (j < last_j))(lambda: step(tm + BF16_ROWS))
        pl.when((i > 0) & (j < last_j))(lambda: step(tm))
        pl.when((i == 0) & (j == last_j))(lambda: last_step(tm + BF16_ROWS))
        pl.when((i > 0) & (j == last_j))(lambda: last_step(tm))
    else:
        pl.when(j < last_j)(lambda: step(tm))
        pl.when(j == last_j)(lambda: last_step(tm))


def _slab_plan(rows, cols, steps):
    best = None
    for cs in (1, 2, 4, 8):
        if cols % (cs * LANES):
            continue
        for rb in range(BF16_ROWS, rows + 1, BF16_ROWS):
            if rows % rb == 0 and (rows // rb) * cs <= steps:
                if best is None or rb * (cols // cs) < best[0] * (cols // best[1]):
                    best = (rb, cs)
                break
    assert best is not None, (rows, cols, steps)
    return best


def _slab_spec(rows, cols, steps, nj, lead):
    rb, cs = _slab_plan(rows, cols, steps)
    last = (rows // rb) * cs - 1

    def index(i, j):
        s = jnp.minimum(i * nj + j, last)
        return s // cs, s % cs

    if lead is None:
        spec_in = pl.BlockSpec((rb, cols // cs), index)
    else:
        spec_in = pl.BlockSpec((None, rb, cols // cs), lambda i, j: (lead,) + index(i, j))
    return spec_in, pl.BlockSpec((rb, cols // cs), index)


def _mlp(x, g, gf, w_up, w_down, *, final_norm, tm, tf, extra=None, casts=()):
    m, d = x.shape
    f = w_up.shape[1]
    tm = min(tm, m)
    tf = min(tf, f)
    ni, nj = m // tm, f // tf
    n_extra = 0 if extra is None else extra.shape[0]
    assert n_extra % SUBLANES == 0 and n_extra <= BF16_ROWS
    const = lambda i, j: (0, 0)
    in_specs = [
        pl.BlockSpec(memory_space=pl.ANY),
        pl.BlockSpec((1, d), const),
        pl.BlockSpec((1, d), const),
        pl.BlockSpec((d, tf), lambda i, j: (0, j)),
        pl.BlockSpec((tf, d), lambda i, j: (j, 0)),
    ]
    out_specs = [pl.BlockSpec(memory_space=pl.ANY)]
    out_shape = [jax.ShapeDtypeStruct((m, d), F32)]
    chunks = MLP_COPY_CHUNKS
    if tm % (chunks * BF16_ROWS) or d % (chunks * LANES):
        chunks = 1
    args = [x, g, gf, w_up, w_down]
    if n_extra:
        in_specs.append(pl.BlockSpec((n_extra, d), const))
        out_specs.append(pl.BlockSpec((n_extra, d), const))
        out_shape.append(jax.ShapeDtypeStruct((n_extra, d), F32))
        args.append(extra)
    for arr, lead in casts:
        rows, cols = arr.shape[-2:]
        spec_in, spec_out = _slab_spec(rows, cols, ni * nj, nj, lead)
        in_specs.append(spec_in)
        out_specs.append(spec_out)
        out_shape.append(jax.ShapeDtypeStruct((rows, cols), BF16))
        args.append(arr)
    outs = pl.pallas_call(
        functools.partial(_mlp_kernel, final_norm=final_norm, n_extra=n_extra, n_cast=len(casts)),
        grid=(ni, nj),
        in_specs=in_specs,
        out_specs=out_specs,
        out_shape=out_shape,
        scratch_shapes=[pltpu.VMEM((tm + (BF16_ROWS if n_extra else 0), d), BF16),
                        pltpu.VMEM((tm, d), F32),
                        pltpu.SemaphoreType.DMA((chunks,)),
                        pltpu.SemaphoreType.DMA((chunks,))],
        compiler_params=_params("arbitrary", "arbitrary"),
        name="mlp",
    )(*args)
    k = 2 if n_extra else 1
    return outs[0], (outs[1] if n_extra else None), list(outs[k:])


def _extra_col_index(nj):
    return lambda i, j: (0, jnp.where(i == 0, j, nj - 1))


def _norm_proj_kernel(x_ref, g_ref, *refs, n, n_extra):
    refs = list(refs)
    xe_ref = refs.pop(0) if n_extra else None
    w_refs = [refs.pop(0) for _ in range(n)]
    o_refs = [refs.pop(0) for _ in range(n)]
    oe_refs = [refs.pop(0) for _ in range(n)] if n_extra else []
    (h_ref,) = refs
    i, j = pl.program_id(0), pl.program_id(1)
    tm = x_ref.shape[0]
    g = g_ref[...]

    @pl.when(j == 0)
    def _():
        def store(rows, h):
            h_ref[rows, :] = h.astype(BF16)
        _rms_rows(x_ref, g, store)

    if n_extra:
        @pl.when((i == 0) & (j == 0))
        def _():
            h_ref[tm:, :] = _pad_rows_bf16(_rms(xe_ref[...], g))

    h = h_ref[:tm, :]
    for w_ref, o_ref in zip(w_refs, o_refs):
        o_ref[...] = jnp.dot(h, w_ref[...], preferred_element_type=F32)

    if n_extra:
        @pl.when(i == 0)
        def _():
            he = h_ref[tm:, :]
            for w_ref, oe_ref in zip(w_refs, oe_refs):
                oe_ref[...] = jnp.dot(he, w_ref[...], preferred_element_type=F32)[:n_extra]


def _norm_proj(x, g, weights, *, tm, tn, extra=None):
    m, d = x.shape
    nout = weights[0].shape[1]
    tm = min(tm, m)
    tn = min(tn, nout)
    n = len(weights)
    ni, nj = m // tm, nout // tn
    n_extra = 0 if extra is None else extra.shape[0]
    assert n_extra % SUBLANES == 0 and n_extra <= BF16_ROWS
    in_specs = [pl.BlockSpec((tm, d), lambda i, j: (i, 0)), pl.BlockSpec((1, d), lambda i, j: (0, 0))]
    out_specs = [pl.BlockSpec((tm, tn), lambda i, j: (i, j))] * n
    out_shape = [jax.ShapeDtypeStruct((m, nout), F32)] * n
    args = [x, g]
    if n_extra:
        in_specs.append(pl.BlockSpec((n_extra, d), lambda i, j: (0, 0)))
        out_specs += [pl.BlockSpec((n_extra, tn), _extra_col_index(nj))] * n
        out_shape += [jax.ShapeDtypeStruct((n_extra, nout), F32)] * n
        args.append(extra)
    in_specs += [pl.BlockSpec((d, tn), lambda i, j: (0, j))] * n
    return pl.pallas_call(
        functools.partial(_norm_proj_kernel, n=n, n_extra=n_extra),
        grid=(ni, nj),
        in_specs=in_specs,
        out_specs=out_specs,
        out_shape=out_shape,
        scratch_shapes=[pltpu.VMEM((tm + (BF16_ROWS if n_extra else 0), d), BF16)],
        compiler_params=_params("arbitrary", "arbitrary"),
        name="norm_proj",
    )(*args, *weights)


def _out_proj_kernel(a_ref, w_ref, x_ref, *refs, n_extra):
    if n_extra:
        ae_ref, xe_ref, o_ref, oe_ref = refs
    else:
        (o_ref,) = refs
    a = a_ref[...].astype(BF16)
    o_ref[...] = x_ref[...] + jnp.dot(a, w_ref[...], preferred_element_type=F32)

    if n_extra:
        @pl.when(pl.program_id(0) == 0)
        def _():
            r = jnp.dot(_pad_rows_bf16(ae_ref[...]), w_ref[...], preferred_element_type=F32)
            oe_ref[...] = xe_ref[...] + r[:n_extra]


def _out_proj(a, w_o, x, *, tm, tn, extra=None):
    m, d = x.shape
    k = a.shape[1]
    tm = min(tm, m)
    tn = min(tn, d)
    ni, nj = m // tm, d // tn
    n_extra = 0 if extra is None else extra[0].shape[0]
    assert n_extra % SUBLANES == 0 and n_extra <= BF16_ROWS
    in_specs = [
        pl.BlockSpec((tm, k), lambda i, j: (i, 0)),
        pl.BlockSpec((k, tn), lambda i, j: (0, j)),
        pl.BlockSpec((tm, tn), lambda i, j: (i, j)),
    ]
    out_specs = [pl.BlockSpec((tm, tn), lambda i, j: (i, j))]
    out_shape = [jax.ShapeDtypeStruct((m, d), F32)]
    args = [a, w_o, x]
    if n_extra:
        in_specs += [pl.BlockSpec((n_extra, k), lambda i, j: (0, 0)),
                     pl.BlockSpec((n_extra, tn), _extra_col_index(nj))]
        out_specs.append(pl.BlockSpec((n_extra, tn), _extra_col_index(nj)))
        out_shape.append(jax.ShapeDtypeStruct((n_extra, d), F32))
        args += list(extra)
    outs = pl.pallas_call(
        functools.partial(_out_proj_kernel, n_extra=n_extra),
        grid=(ni, nj),
        in_specs=in_specs,
        out_specs=out_specs,
        out_shape=out_shape,
        compiler_params=_params("arbitrary", "arbitrary"),
        name="out_proj",
    )(*args)
    return outs[0], (outs[1] if n_extra else None)


def _window_sums(ext, w, rows):
    s, span = ext, 1
    while span < w:
        s = s[span:] + s[:-span]
        span *= 2
    start = HALO + 1 - w
    return s[start:start + rows]


def _pool_prompt_kernel(x_ref, halo_ref, g_ref, w_ref, s_ref, o_ref, st_ref, *, tp):
    i = pl.program_id(1)
    x = x_ref[0]
    g = g_ref[...]
    h = _rms(x, g)
    hh = _rms(halo_ref[0], g)
    hh = jnp.where(i > 0, hh, 0.0)
    ext = jnp.concatenate([hh, h], axis=0)
    cg = x.shape[1] // len(POOL_WINDOWS)
    pos = i * tp + lax.broadcasted_iota(jnp.int32, (tp, 1), 0)
    for gi, w in enumerate(POOL_WINDOWS):
        c0, c1 = gi * cg, (gi + 1) * cg
        wsum = _window_sums(ext[:, c0:c1], w, tp)
        cnt = jnp.minimum(pos + 1, w).astype(F32)
        dlt = wsum * (1.0 / cnt) - h[:, c0:c1]
        y = jnp.dot(dlt.astype(BF16), w_ref[gi], preferred_element_type=F32)
        o_ref[0, :, c0:c1] = x[:, c0:c1] + y * s_ref[:, c0:c1]

    @pl.when(i == pl.num_programs(1) - 1)
    def _():
        st_ref[0] = h[tp - POOL_STATE:, :]


def _pool_prompt(x, g, w_pool, layer, s, *, tp):
    b, t, d = x.shape
    tp = min(tp, t)
    cg = d // len(POOL_WINDOWS)
    hb = tp // HALO
    return pl.pallas_call(
        functools.partial(_pool_prompt_kernel, tp=tp),
        grid=(b, t // tp),
        in_specs=[
            pl.BlockSpec((1, tp, d), lambda bi, i: (bi, i, 0)),
            pl.BlockSpec((1, HALO, d), lambda bi, i: (bi, jnp.maximum(i * hb - 1, 0), 0)),
            pl.BlockSpec((1, d), lambda bi, i: (0, 0)),
            pl.BlockSpec((None, len(POOL_WINDOWS), cg, cg), lambda bi, i: (layer, 0, 0, 0)),
            pl.BlockSpec((1, d), lambda bi, i: (0, 0)),
        ],
        out_specs=[
            pl.BlockSpec((1, tp, d), lambda bi, i: (bi, i, 0)),
            pl.BlockSpec((1, POOL_STATE, d), lambda bi, i: (bi, 0, 0)),
        ],
        out_shape=[jax.ShapeDtypeStruct((b, t, d), F32),
                   jax.ShapeDtypeStruct((b, POOL_STATE, d), F32)],
        compiler_params=_params("parallel", "arbitrary"),
        name="pool_prompt",
    )(x, x, g, w_pool, s)


def _pool_sample_kernel(x_ref, prev_ref, g_ref, w_ref, s_ref, o_ref, st_ref, d_ref, *, pos0):
    x = x_ref[...]
    h = _rms(x, g_ref[...])
    nb = x.shape[0]
    cg = x.shape[1] // len(POOL_WINDOWS)
    for b in range(nb):
        pv = prev_ref[b]
        hb = h[b:b + 1]
        st_ref[b, 0:POOL_STATE - 1, :] = pv[1:POOL_STATE]
        st_ref[b, POOL_STATE - 1:POOL_STATE, :] = hb
        for gi, w in enumerate(POOL_WINDOWS):
            c0, c1 = gi * cg, (gi + 1) * cg
            wsum = hb[:, c0:c1] + jnp.sum(pv[POOL_STATE + 1 - w:, c0:c1], axis=0, keepdims=True)
            cnt = float(min(pos0 + 1, w))
            d_ref[b:b + 1, c0:c1] = wsum / cnt - hb[:, c0:c1]
    for gi in range(len(POOL_WINDOWS)):
        c0, c1 = gi * cg, (gi + 1) * cg
        y = jnp.dot(d_ref[:, c0:c1].astype(BF16), w_ref[gi], preferred_element_type=F32)
        o_ref[:, c0:c1] = x[:, c0:c1] + y * s_ref[:, c0:c1]


def _pool_sample(x, prev, g, w_pool, layer, s, *, pos0):
    b, d = x.shape
    cg = d // len(POOL_WINDOWS)
    return pl.pallas_call(
        functools.partial(_pool_sample_kernel, pos0=pos0),
        grid=(1,),
        in_specs=[
            pl.BlockSpec((b, d), lambda i: (0, 0)),
            pl.BlockSpec((None, b, POOL_STATE, d), lambda i: (layer, 0, 0, 0)),
            pl.BlockSpec((1, d), lambda i: (0, 0)),
            pl.BlockSpec((None, len(POOL_WINDOWS), cg, cg), lambda i: (layer, 0, 0, 0)),
            pl.BlockSpec((1, d), lambda i: (0, 0)),
        ],
        out_specs=[
            pl.BlockSpec((b, d), lambda i: (0, 0)),
            pl.BlockSpec((b, POOL_STATE, d), lambda i: (0, 0, 0)),
        ],
        out_shape=[jax.ShapeDtypeStruct((b, d), F32),
                   jax.ShapeDtypeStruct((b, POOL_STATE, d), F32)],
        scratch_shapes=[pltpu.VMEM((b, d), F32)],
        compiler_params=_params("arbitrary"),
        name="pool_sample",
    )(x, prev, g, w_pool, s)


ROW_CHUNK = 64


def _moba_prompt_kernel(q_ref, k_ref, v_ref, o_ref, *refs, scale, emit_kv):
    if emit_kv:
        k4_ref, v4_ref, qa_ref, ka_ref, va_ref, s_ref, p_ref, kv_sem = refs
        bi, hi = pl.program_id(0), pl.program_id(1)
        kv_copies = [pltpu.make_async_copy(k_ref.at[0], k4_ref.at[bi, :, hi, :], kv_sem.at[0]),
                     pltpu.make_async_copy(v_ref.at[0], v4_ref.at[bi, :, hi, :], kv_sem.at[1])]
        for cp in kv_copies:
            cp.start()
    else:
        qa_ref, ka_ref, va_ref, s_ref, p_ref = refs
    t, dh = q_ref.shape[1], q_ref.shape[2]
    blk = MOBA_BLOCK
    nb = t // blk
    nbp = -(-nb // SUBLANES) * SUBLANES
    cexp = scale * LOG2E

    k = k_ref[0]
    lane_t = lax.broadcasted_iota(jnp.int32, (t, LANES), 1)
    ka_ref[:dh, :] = k.T.astype(BF16)
    mask_row = lax.broadcasted_iota(jnp.int32, (LANES, t), 0)
    mask_key = lax.broadcasted_iota(jnp.int32, (LANES, t), 1) // blk
    ka_ref[dh:, :] = jnp.where(mask_row == mask_key, -MASK_BIAS, 0.0).astype(BF16)
    va_ref[:, :dh] = v_ref[0].astype(BF16)
    va_ref[:, dh:] = jnp.where(lane_t == 0, 1.0, 0.0).astype(BF16)
    means = [jnp.mean(k[n * blk:(n + 1) * blk], axis=0, keepdims=True) for n in range(nb)]
    if nbp > nb:
        means.append(jnp.zeros((nbp - nb, dh), F32))
    kmean = jnp.concatenate(means, axis=0)

    km_hi = kmean.astype(BF16)
    km_lo = kmean - km_hi.astype(F32)
    km_parts = jnp.concatenate([km_hi.astype(F32), km_lo], axis=0).astype(BF16)

    n_iota = lax.broadcasted_iota(jnp.int32, (nbp, blk), 0)
    lane_q = lax.broadcasted_iota(jnp.int32, (blk, LANES), 1)
    row_c = lax.broadcasted_iota(jnp.int32, (ROW_CHUNK, blk), 0)
    col_c = lax.broadcasted_iota(jnp.int32, (ROW_CHUNK, blk), 1)

    def augment(c):
        buf = c
        qc = q_ref[0, c * blk:(c + 1) * blk, :]
        q_hi = qc.astype(BF16)
        qa_ref[buf, :, :dh] = q_hi
        if c > 0:
            q_lo = (qc - q_hi.astype(F32)).astype(BF16)
            g_hi = lax.dot_general(km_parts, q_hi, _NT, preferred_element_type=F32)
            g_lo = lax.dot_general(km_parts, q_lo, _NT, preferred_element_type=F32)
            gt = g_hi[:nbp] + g_hi[nbp:] + g_lo[:nbp]
            past = n_iota < c
            gt = jnp.where(past, gt, NEG_INF)
            rank = jnp.zeros((nbp, blk), jnp.int32)
            for m in range(c):
                rm = gt[m:m + 1, :]
                beats = (rm > gt) | ((rm == gt) & (n_iota > m))
                rank = rank + beats.astype(jnp.int32)
            sel_t = past & (rank < MOBA_TOPK) & (jnp.abs(gt) < INF)
            sel_t = jnp.concatenate(
                [sel_t.astype(F32), jnp.zeros((LANES - nbp, blk), F32)], axis=0)
            sel = sel_t.T
            qa_ref[buf, :, dh:] = jnp.where(lane_q < c, 1.0 - sel, 0.0).astype(BF16)
        else:
            qa_ref[buf, :, dh:] = jnp.zeros((blk, LANES), BF16)

    def scores(c):
        nk = (c + 1) * blk
        s_ref[c % 2, :, :nk] = jnp.dot(qa_ref[c], ka_ref[:, :nk], preferred_element_type=F32)

    def weighted_values(c):
        nk = (c + 1) * blk
        o = jnp.dot(p_ref[c % 2, :, :nk], va_ref[:nk, :], preferred_element_type=F32)
        o_ref[0, c * blk:(c + 1) * blk, :] = (o[:, :dh] * (1.0 / o[:, dh:dh + 1])).astype(o_ref.dtype)

    for c in range(nb):
        augment(c)
    scores(0)
    for c in range(nb):
        buf = c % 2
        nk = (c + 1) * blk
        if c + 1 < nb:
            scores(c + 1)
        if c > 0:
            weighted_values(c - 1)
        for r in range(blk // ROW_CHUNK):
            r0 = r * ROW_CHUNK
            causal = col_c <= row_c + r0
            sd = jnp.where(causal, s_ref[buf, r0:r0 + ROW_CHUNK, c * blk:nk], -MASK_BIAS)
            mx = sd
            for n in range(c):
                mx = jnp.maximum(mx, s_ref[buf, r0:r0 + ROW_CHUNK, n * blk:(n + 1) * blk])
            mb = jnp.broadcast_to(jnp.max(mx, axis=1, keepdims=True), (ROW_CHUNK, blk))
            p_ref[buf, r0:r0 + ROW_CHUNK, c * blk:nk] = jnp.exp2((sd - mb) * cexp).astype(BF16)
            for n in range(c):
                sn = s_ref[buf, r0:r0 + ROW_CHUNK, n * blk:(n + 1) * blk]
                p_ref[buf, r0:r0 + ROW_CHUNK, n * blk:(n + 1) * blk] = (
                    jnp.exp2((sn - mb) * cexp).astype(BF16))
    weighted_values(nb - 1)
    if emit_kv:
        for cp in kv_copies:
            cp.wait()


def _moba_prompt(q, k, v, *, heads, emit_kv=False):
    b, t, hd = q.shape
    dh = hd // heads
    spec = pl.BlockSpec((1, t, dh), lambda bi, h: (bi, 0, h))
    out_specs = [spec]
    out_shape = [jax.ShapeDtypeStruct((b, t, hd), BF16)]
    scratch = [
        pltpu.VMEM((t // MOBA_BLOCK, MOBA_BLOCK, dh + LANES), BF16),
        pltpu.VMEM((dh + LANES, t), BF16),
        pltpu.VMEM((t, dh + LANES), BF16),
        pltpu.VMEM((2, MOBA_BLOCK, t), F32),
        pltpu.VMEM((2, MOBA_BLOCK, t), BF16),
    ]
    if emit_kv:
        out_specs += [pl.BlockSpec(memory_space=pl.ANY)] * 2
        out_shape += [jax.ShapeDtypeStruct((b, t, heads, dh), k.dtype)] * 2
        scratch.append(pltpu.SemaphoreType.DMA((2,)))
    outs = pl.pallas_call(
        functools.partial(_moba_prompt_kernel, scale=dh ** -0.5, emit_kv=emit_kv),
        grid=(b, heads),
        in_specs=[spec, spec, spec],
        out_specs=out_specs,
        out_shape=out_shape,
        scratch_shapes=scratch,
        compiler_params=_params("arbitrary", "arbitrary"),
        name="moba_prompt",
    )(q, k, v)
    return outs if emit_kv else outs[0]


def _kmean_kernel(pt_ref, *refs, ppb):
    k_refs, o_ref = refs[:ppb], refs[ppb]
    n = pl.program_id(1)
    acc = jnp.sum(k_refs[0][0], axis=0)
    for r in k_refs[1:]:
        acc = acc + jnp.sum(r[0], axis=0)
    rows = ppb * k_refs[0].shape[1]
    o_ref[0, pl.ds(n, 1)] = (acc * (1.0 / rows))[None]


def _kmean(page_table_flat, cache_k, *, batch, n_pages, ppb):
    _, page, heads, dh = cache_k.shape
    nbp = n_pages // ppb
    specs = [
        pl.BlockSpec((1, page, heads, dh),
                     lambda b, n, pt, p=p: (pt[b * n_pages + n * ppb + p], 0, 0, 0))
        for p in range(ppb)
    ]
    return pl.pallas_call(
        functools.partial(_kmean_kernel, ppb=ppb),
        grid_spec=pltpu.PrefetchScalarGridSpec(
            num_scalar_prefetch=1,
            grid=(batch, nbp),
            in_specs=specs,
            out_specs=pl.BlockSpec((1, nbp, heads, dh), lambda b, n, pt: (b, 0, 0, 0)),
        ),
        out_shape=jax.ShapeDtypeStruct((batch, nbp, heads, dh), F32),
        compiler_params=_params("parallel", "arbitrary"),
        name="kmean",
    )(page_table_flat, *([cache_k] * ppb))


def _gate_topk_kernel(q_ref, km_ref, idx_ref, ok_ref):
    nbt = q_ref.shape[0]
    nbp, heads = km_ref.shape[1], km_ref.shape[2]
    n_iota = lax.broadcasted_iota(jnp.int32, (nbp, heads, 1), 0)
    for b in range(nbt):
        g = jnp.sum(km_ref[b] * q_ref[b][None], axis=-1, keepdims=True)
        for s in range(MOBA_TOPK):
            mx = jnp.max(g, axis=0)
            idx = jnp.min(jnp.where(g == mx[None], n_iota, nbp), axis=0)
            idx = jnp.minimum(idx, nbp - 1)
            idx_ref[b, s] = idx
            ok_ref[b, s] = (jnp.abs(mx) < INF).astype(jnp.int32)
            g = jnp.where(n_iota == idx[None], NEG_INF, g)


def _gate_topk(q3, kmean):
    b, heads, dh = q3.shape
    nbp = kmean.shape[1]
    out = jax.ShapeDtypeStruct((b, MOBA_TOPK, heads, 1), jnp.int32)
    out_spec = pl.BlockSpec((b, MOBA_TOPK, heads, 1), lambda i: (0, 0, 0, 0))
    return pl.pallas_call(
        _gate_topk_kernel,
        grid=(1,),
        in_specs=[pl.BlockSpec((b, heads, dh), lambda i: (0, 0, 0)),
                  pl.BlockSpec((b, nbp, heads, dh), lambda i: (0, 0, 0, 0))],
        out_specs=[out_spec, out_spec],
        out_shape=[out, out],
        compiler_params=_params("arbitrary"),
        name="gate_topk",
    )(q3, kmean)


GATHER_SLOTS = 8


def _moba_sample_kernel(pt_ref, idx_ref, ok_ref, q_ref, kn_ref, vn_ref, ck_ref, cv_ref, o_ref,
                        kbuf, vbuf, sem, *, heads, n_pages, ppb, scale):
    n_pairs = q_ref.shape[0]
    nsp, page, dh = kbuf.shape[1], kbuf.shape[2], kbuf.shape[3]
    rows = nsp * page

    def gather(pair, slot):
        b, h = pair // heads, pair % heads
        cps = []
        for s in range(MOBA_TOPK):
            blk = idx_ref[(b * MOBA_TOPK + s) * heads + h]
            for p in range(ppb):
                phys = pt_ref[b * n_pages + blk * ppb + p]
                j = s * ppb + p
                cps.append(pltpu.make_async_copy(ck_ref.at[phys, :, h, :], kbuf.at[slot, j], sem.at[0, slot]))
                cps.append(pltpu.make_async_copy(cv_ref.at[phys, :, h, :], vbuf.at[slot, j], sem.at[1, slot]))
        return cps

    n_slots = kbuf.shape[0]
    ahead = n_slots - 1
    for t in range(min(ahead, n_pairs)):
        for cp in gather(t, t):
            cp.start()
    row_blk = lax.broadcasted_iota(jnp.int32, (rows, 1), 0) // MOBA_BLOCK

    def body(pair, carry):
        slot = pair % n_slots

        @pl.when(pair + ahead < n_pairs)
        def _():
            for cp in gather(pair + ahead, (pair + ahead) % n_slots):
                cp.start()

        for cp in gather(pair, slot):
            cp.wait()
        b, h = pair // heads, pair % heads
        kk = kbuf[slot].reshape(rows, dh)
        vv = vbuf[slot].reshape(rows, dh)
        q = q_ref[pl.ds(pair, 1), :]
        s = jnp.sum(kk * q, axis=1, keepdims=True) * scale
        for j in range(MOBA_TOPK):
            ok = ok_ref[(b * MOBA_TOPK + j) * heads + h]
            s = s + jnp.where(row_blk == j, jnp.where(ok == 0, NEG_INF, 0.0), 0.0)
        s_own = jnp.sum(q * kn_ref[pl.ds(pair, 1), :], axis=1, keepdims=True) * scale
        mx = jnp.maximum(jnp.max(s, axis=0, keepdims=True), s_own)
        p = jnp.exp(s - mx)
        p_own = jnp.exp(s_own - mx)
        l = jnp.sum(p, axis=0, keepdims=True) + p_own
        o = jnp.sum(p * vv, axis=0, keepdims=True) + p_own * vn_ref[pl.ds(pair, 1), :]
        o_ref[pl.ds(pair, 1), :] = o / l
        return carry

    lax.fori_loop(0, n_pairs, body, 0)


def _moba_sample(page_table_flat, idx_flat, ok_flat, q2, kn2, vn2, cache_k, cache_v, *,
                 heads, n_pages, ppb):
    n_pairs, dh = q2.shape
    page = cache_k.shape[1]
    nsp = MOBA_TOPK * ppb
    full = pl.BlockSpec((n_pairs, dh), lambda i, pt, ix, ok: (0, 0))
    hbm = pl.BlockSpec(memory_space=pl.ANY)
    return pl.pallas_call(
        functools.partial(_moba_sample_kernel, heads=heads, n_pages=n_pages, ppb=ppb,
                          scale=dh ** -0.5),
        grid_spec=pltpu.PrefetchScalarGridSpec(
            num_scalar_prefetch=3,
            grid=(1,),
            in_specs=[full, full, full, hbm, hbm],
            out_specs=full,
            scratch_shapes=[
                pltpu.VMEM((GATHER_SLOTS, nsp, page, dh), F32),
                pltpu.VMEM((GATHER_SLOTS, nsp, page, dh), F32),
                pltpu.SemaphoreType.DMA((2, GATHER_SLOTS)),
            ],
        ),
        out_shape=jax.ShapeDtypeStruct((n_pairs, dh), F32),
        compiler_params=_params("arbitrary"),
        name="moba_sample",
    )(page_table_flat, idx_flat, ok_flat, q2, kn2, vn2, cache_k, cache_v)


def kernel(x_prompt, x_sample, cache_k, cache_v, state_pool, page_table, g_pool, w_pool, s_pool,
           g_mlp, w_up, w_down, g_kv, w_k, w_v, g_attn, w_q, w_o, g_final):
    b, t, d = x_prompt.shape
    bs, ts, _ = x_sample.shape
    n_phys, page, heads, dh = cache_k.shape
    hd = heads * dh
    depth = w_up.shape[0]
    n_pool = g_pool.shape[0]
    n_pages = page_table.shape[1]
    past_len = n_pages * page
    assert ts == 1 and MOBA_BLOCK % page == 0 and past_len % MOBA_BLOCK == 0
    assert t % MOBA_BLOCK == 0 and t >= POOL_STATE and dh == LANES
    assert past_len // MOBA_BLOCK >= MOBA_TOPK and heads <= LANES
    ppb = MOBA_BLOCK // page

    assert 1 <= n_pool < depth
    n_attn = depth - n_pool
    row = lambda a: a.reshape(1, -1)

    w_pool_b = w_pool.astype(BF16)
    wu_b = [w_up[0].astype(BF16)] + [None] * (depth - 1)
    wd_b = [w_down[0].astype(BF16)] + [None] * (depth - 1)
    side = [[(w_up, l + 1), (w_down, l + 1)] for l in range(depth - 1)]
    side[max(n_pool - 2, 0)] += [(w_k, None), (w_v, None)]
    for j in range(n_attn):
        side[n_pool + j - 1] += [(w_q, j), (w_o, j)]
    w_k_b = w_v_b = None
    w_q_b, w_o_b = [None] * n_attn, [None] * n_attn

    pt_flat = page_table.reshape(-1).astype(jnp.int32)
    kmean = _kmean(pt_flat, cache_k, batch=bs, n_pages=n_pages, ppb=ppb)

    xf = x_prompt.reshape(b * t, d)
    xs = x_sample.reshape(bs, d)
    pool_p, pool_s = [], []
    kp = vp = ks = vs = None
    for l in range(depth):
        if l < n_pool:
            xp, st = _pool_prompt(xf.reshape(b, t, d), row(g_pool[l]), w_pool_b, l, row(s_pool[l]),
                                  tp=512)
            pool_p.append(st)
            xf = xp.reshape(b * t, d)
            xs, st = _pool_sample(xs, state_pool, row(g_pool[l]), w_pool_b, l, row(s_pool[l]),
                                  pos0=past_len)
            pool_s.append(st)
        else:
            j = l - n_pool
            if j == 0:
                q, qs = q_first
            else:
                q, qs = _norm_proj(xf, row(g_attn[j]), [w_q_b[j]], tm=512, tn=1024, extra=xs)
            if j == 0:
                a, k_out, v_out = _moba_prompt(q.reshape(b, t, hd), kp, vp, heads=heads, emit_kv=True)
            else:
                a = _moba_prompt(q.reshape(b, t, hd), kp, vp, heads=heads)
            idx, ok = _gate_topk(qs.reshape(bs, heads, dh), kmean)
            a_s = _moba_sample(pt_flat, idx.reshape(-1), ok.reshape(-1), qs.reshape(bs * heads, dh),
                               ks.reshape(bs * heads, dh), vs.reshape(bs * heads, dh),
                               cache_k, cache_v, heads=heads, n_pages=n_pages, ppb=ppb)
            xf, xs = _out_proj(a.reshape(b * t, hd), w_o_b[j], xf, tm=1024, tn=1024,
                               extra=(a_s.reshape(bs, hd), xs))

        last = l == depth - 1
        jobs = [] if last else side[l]
        xf, xs, rounded = _mlp(xf, row(g_mlp[l]), row(g_final), wu_b[l], wd_b[l],
                               final_norm=last, tm=512, tf=1024, extra=xs, casts=jobs)
        for (src, lead), w_b in zip(jobs, rounded):
            if src is w_up:
                wu_b[lead] = w_b
            elif src is w_down:
                wd_b[lead] = w_b
            elif src is w_k:
                w_k_b = w_b
            elif src is w_v:
                w_v_b = w_b
            elif src is w_q:
                w_q_b[lead] = w_b
            else:
                w_o_b[lead] = w_b

        if l == n_pool - 1:
            q_first = _norm_proj(xf, row(g_attn[0]), [w_q_b[0]], tm=512, tn=1024, extra=xs)
            kf, vf, ks, vs = _norm_proj(xf, row(g_kv), [w_k_b, w_v_b], tm=512, tn=512, extra=xs)
            kp, vp = kf.reshape(b, t, hd), vf.reshape(b, t, hd)
    y_prompt = xf.reshape(b, t, d)
    y_sample = xs.reshape(bs, ts, d)

    return (y_prompt, y_sample,
            k_out, v_out, jnp.stack(pool_p, axis=0),
            ks.reshape(bs, ts, heads, dh), vs.reshape(bs, ts, heads, dh), jnp.stack(pool_s, axis=0))
```
